```python
import math
import jax
import jax.numpy as jnp
from jax import lax
import numpy as np

D_MODEL = 2048
BATCH = 2
SEQ = 16384
DEPTH = 4

GRID_W = 64
CTX_LEN = 256

D_HY = D_MODEL // 2
D_HG = D_MODEL // 2
HG_EXPAND = 128
HG_HEADS = D_HG // HG_EXPAND
HG_DV = D_HG // HG_HEADS
HG_CHUNK = 64
HY_SHORT = 3
HY_EMB = 33
HY_BANDS = (HY_EMB - 1) // 2
HY_FH = 64
HY_MAX_DECAY = math.log(1e-2) / 0.3
HY_MIN_DECAY = math.log(1e-2) / 1.5
N_EXPERTS = 16
N_GROUPS = 4
TOP_K = 2
D_FF = 1024
MOE_BLOCK = 256
RMS_EPS = 1e-6

OFF_HG = 3 * D_HY
OFF_GATE = OFF_HG + 5 * D_HG
N_IN = OFF_GATE + 2 * D_MODEL

kernel_name = "hyena_hgrn2_moe_prefix_dit"


def _rmsnorm(x, g):
    xf = x.astype(jnp.float32)
    y = xf * lax.rsqrt(jnp.mean(xf * xf, axis=-1, keepdims=True) + RMS_EPS)
    return (y * g.astype(jnp.float32)).astype(x.dtype)


def _modulate(h, shift, scale):
    return (h.astype(jnp.float32) * (1.0 + scale) + shift).astype(h.dtype)


def _short_conv(u, w, b):
    L = u.shape[1]
    pad = HY_SHORT // 2
    up = jnp.pad(u, ((0, 0), (pad, pad), (0, 0)))
    y = up[:, 0:L] * w[0]
    for j in range(1, HY_SHORT):
        y = y + up[:, j:j + L] * w[j]
    return y + b


def _hyena_kernel_fft(L, fw1, fb1, fw2, fb2, fw3, fb3, fwout, freq):
    f32 = jnp.float32
    t = jnp.linspace(0.0, 1.0, L, dtype=f32)[:, None]
    w = (2.0 * math.pi / L) * jnp.arange(L, dtype=f32)[:, None]
    fb = jnp.linspace(1e-4, HY_BANDS - 1, HY_BANDS, dtype=f32)[None, :]
    z = jnp.concatenate([t, jnp.cos(fb * w), -jnp.sin(fb * w)], axis=-1)
    h = jnp.sin(freq[0].astype(f32) * (z @ fw1.astype(f32) + fb1.astype(f32)))
    h = jnp.sin(freq[1].astype(f32) * (h @ fw2.astype(f32) + fb2.astype(f32)))
    h = jnp.sin(freq[2].astype(f32) * (h @ fw3.astype(f32) + fb3.astype(f32)))
    h = (h @ fwout.astype(f32)).reshape(L, 2, 2 * D_HY)
    deltas = jnp.abs(jnp.linspace(HY_MIN_DECAY, HY_MAX_DECAY, 2 * D_HY, dtype=f32))
    h = h * jnp.exp(-t[:, :, None] * deltas)
    k = jnp.concatenate([h[:, 0], jnp.zeros((1, 2 * D_HY), f32), h[:0:-1, 1]], axis=0)
    k = k / jnp.sum(jnp.abs(k), axis=0, keepdims=True)
    return jnp.fft.rfft(k, axis=0)


def _long_conv(u, kf, bias):
    L = u.shape[1]
    uf32 = u.astype(jnp.float32)
    uf = jnp.fft.rfft(uf32, n=2 * L, axis=1)
    y = jnp.fft.irfft(uf * kf[None], n=2 * L, axis=1)[:, :L]
    return y + uf32 * bias.astype(jnp.float32)


def _hyena(p, conv_w, conv_b, kf, bias):
    u = _short_conv(p, conv_w, conv_b).astype(jnp.float32)
    v, x1, x2 = jnp.split(u, 3, axis=-1)
    z = x1 * _long_conv(v, kf[:, :D_HY], bias[0])
    y = x2 * _long_conv(z, kf[:, D_HY:], bias[1])
    return y.astype(p.dtype)


def _heads(a):
    B, L, _ = a.shape
    return a.reshape(B, L, HG_HEADS, -1).astype(jnp.float32)


def _forget(fz, lb):
    f = lb + (1.0 - lb) * jax.nn.sigmoid(fz.astype(jnp.float32))
    return _heads(1.0 - f), _heads(jnp.log(f))


def _hgrn2_kv(p_ifb, lb_layer):
    i, ff, fb = jnp.split(p_ifb, 3, axis=-1)
    return _heads(i), _forget(ff, lb_layer[0]), _forget(fb, lb_layer[1])


def _chunk_scan(q, k, v, g, s0):
    B, L, H, DK = q.shape
    DV = v.shape[-1]
    n = L // HG_CHUNK

    def chunks(a):
        return a.reshape(B, n, HG_CHUNK, H, a.shape[-1]).transpose(1, 0, 3, 2, 4)

    lower = jnp.tril(jnp.ones((HG_CHUNK, HG_CHUNK), dtype=bool))[:, :, None]

    def step(S, inp):
        qc, kc, vc, gc = inp
        b = jnp.cumsum(gc, axis=2)
        o_inter = jnp.einsum("bhtk,bhkv->bhtv", qc * jnp.exp(b), S)
        diff = jnp.where(lower, b[:, :, :, None, :] - b[:, :, None, :, :], -jnp.inf)
        att = jnp.einsum("bhtk,bhsk,bhtsk->bhts", qc, kc, jnp.exp(diff))
        o = o_inter + jnp.einsum("bhts,bhsv->bhtv", att, vc)
        b_end = b[:, :, -1:, :]
        S = jnp.exp(b_end[:, :, 0, :, None]) * S + jnp.einsum("bhsk,bhsv->bhkv", kc * jnp.exp(b_end - b), vc)
        return S, o

    S, o = lax.scan(step, s0, (chunks(q), chunks(k), chunks(v), chunks(g)))
    return o.transpose(1, 0, 3, 2, 4).reshape(B, L, H, DV), S


def _final_state(k, v, g):
    b = jnp.cumsum(g, axis=1)
    return jnp.einsum("blhk,blhv->bhkv", k * jnp.exp(b[:, -1:] - b), v)


def _rev(a):
    return jnp.flip(a, axis=1)


def _bidir(q, v, kg_f, kg_b, s_f, s_b):
    o_f, S_f = _chunk_scan(q, kg_f[0], v, kg_f[1], s_f)
    o_b, S_b = _chunk_scan(_rev(q), _rev(kg_b[0]), _rev(v), _rev(kg_b[1]), s_b)
    return o_f + _rev(o_b), S_f, S_b


def _hgrn2_readout(o, og, norm_g):
    on = o * lax.rsqrt(jnp.mean(o * o, axis=-1, keepdims=True) + RMS_EPS) * norm_g.astype(jnp.float32)
    B, L = on.shape[:2]
    return (on.reshape(B, L, D_HG) * jax.nn.silu(og.astype(jnp.float32))).astype(og.dtype)


def _merge(p, y_hy, y_hg, p_hy, p_hg, w_out):
    f32 = jnp.float32
    g_hy = jax.nn.sigmoid(p[..., OFF_GATE:OFF_GATE + D_MODEL].astype(f32))
    g_hg = jax.nn.sigmoid(p[..., OFF_GATE + D_MODEL:N_IN].astype(f32))
    u = g_hy * (y_hy @ p_hy).astype(f32) + g_hg * (y_hg @ p_hg).astype(f32)
    return u.astype(p.dtype) @ w_out


def _moe(h, router_w, router_b, w1, w3, w2):
    N, D = h.shape
    f32 = jnp.float32
    scores = jax.nn.sigmoid(h.astype(f32) @ router_w.astype(f32))
    sel = scores + router_b.astype(f32)
    per = N_EXPERTS // N_GROUPS
    grp_score = lax.top_k(sel.reshape(N, N_GROUPS, per), TOP_K)[0].sum(-1)
    g_best = jnp.argmax(grp_score, axis=-1)
    in_grp = (jnp.arange(N_EXPERTS) // per)[None, :] == g_best[:, None]
    _, idx = lax.top_k(jnp.where(in_grp, sel, -jnp.inf), TOP_K)
    wts = jnp.take_along_axis(scores, idx, axis=-1)
    wts = wts / jnp.sum(wts, axis=-1, keepdims=True)

    A = N * TOP_K
    e_flat = idx.reshape(-1).astype(jnp.int32)
    tok_flat = jnp.arange(A, dtype=jnp.int32) // TOP_K
    w_flat = wts.reshape(-1)
    order = jnp.argsort(e_flat)
    e_s, tok_s, w_s = e_flat[order], tok_flat[order], w_flat[order]
    counts = jnp.zeros((N_EXPERTS,), jnp.int32).at[e_flat].add(1)
    padded = (counts + MOE_BLOCK - 1) // MOE_BLOCK * MOE_BLOCK
    pend = jnp.cumsum(padded)
    pstart = pend - padded
    sstart = jnp.cumsum(counts) - counts
    pos = pstart[e_s] + jnp.arange(A, dtype=jnp.int32) - sstart[e_s]
    P = (A + N_EXPERTS * (MOE_BLOCK - 1) + MOE_BLOCK - 1) // MOE_BLOCK * MOE_BLOCK
    n_blk = P // MOE_BLOCK
    buf_tok = jnp.full((P,), N, jnp.int32).at[pos].set(tok_s)
    buf_w = jnp.zeros((P,), f32).at[pos].set(w_s)
    blk_e = jnp.minimum(jnp.searchsorted(pend, jnp.arange(n_blk, dtype=jnp.int32) * MOE_BLOCK, side="right"),
                        N_EXPERTS - 1).astype(jnp.int32)
    h_pad = jnp.concatenate([h, jnp.zeros((1, D), h.dtype)], axis=0)
    xs = h_pad[buf_tok].reshape(n_blk, MOE_BLOCK, D)

    def expert_block(args):
        xb, e = args
        return (jax.nn.silu(xb @ w1[e]) * (xb @ w3[e])) @ w2[e]

    ys = lax.map(expert_block, (xs, blk_e)).reshape(P, D)
    out = jnp.zeros((N + 1, D), ys.dtype).at[buf_tok].add(ys * buf_w[:, None].astype(ys.dtype))
    return out[:N]


def setup_inputs(seed: int = 0) -> dict:
    key = jax.random.key(seed)
    ks = jax.random.split(key, 32)
    f32 = jnp.float32
    D = D_MODEL

    def nrm(k, shape, s):
        return jax.random.normal(k, shape, f32) * s

    return {
        "x": nrm(ks[0], (BATCH, SEQ, D), 1.0),
        "c": nrm(ks[1], (BATCH, D), 1.0),
        "ctx": nrm(ks[2], (BATCH, CTX_LEN, D), 1.0),
        "c_ctx": nrm(ks[3], (D,), 1.0),
        "ada_w": nrm(ks[4], (DEPTH, D, 6 * D), 0.5 * D ** -0.5),
        "ada_b": nrm(ks[5], (DEPTH, 6 * D), 0.01),
        "norm1_g": 1.0 + nrm(ks[6], (DEPTH, D), 0.05),
        "norm2_g": 1.0 + nrm(ks[7], (DEPTH, D), 0.05),
        "final_g": 1.0 + nrm(ks[8], (D,), 0.05),
        "w_in": nrm(ks[9], (DEPTH, D, N_IN), D ** -0.5),
        "hy_conv_w": nrm(ks[10], (DEPTH, HY_SHORT, 3 * D_HY), HY_SHORT ** -0.5),
        "hy_conv_b": nrm(ks[11], (DEPTH, 3 * D_HY), 0.01),
        "hy_fw1": nrm(ks[12], (DEPTH, HY_EMB, HY_FH), HY_EMB ** -0.5),
        "hy_fb1": nrm(ks[13], (DEPTH, HY_FH), 0.1),
        "hy_fw2": nrm(ks[14], (DEPTH, HY_FH, HY_FH), HY_FH ** -0.5),
        "hy_fb2": nrm(ks[15], (DEPTH, HY_FH), 0.1),
        "hy_fw3": nrm(ks[16], (DEPTH, HY_FH, HY_FH), HY_FH ** -0.5),
        "hy_fb3": nrm(ks[17], (DEPTH, HY_FH), 0.1),
        "hy_fwout": nrm(ks[18], (DEPTH, HY_FH, 4 * D_HY), HY_FH ** -0.5),
        "hy_freq": 1.0 + nrm(ks[19], (DEPTH, 3, HY_FH), 0.1),
        "hy_bias": nrm(ks[20], (DEPTH, 2, D_HY), 0.2),
        "hg_lb_raw": nrm(ks[21], (DEPTH, 2, D_HG), 0.1),
        "hg_norm_g": 1.0 + nrm(ks[22], (DEPTH, HG_DV), 0.05),
        "p_hy": nrm(ks[23], (DEPTH, D_HY, D), D_HY ** -0.5),
        "p_hg": nrm(ks[24], (DEPTH, D_HG, D), D_HG ** -0.5),
        "w_out": nrm(ks[25], (DEPTH, D, D), D ** -0.5),
        "router_w": nrm(ks[26], (D, N_EXPERTS), D ** -0.5),
        "router_b": nrm(ks[27], (N_EXPERTS,), 0.01),
        "moe_w1": nrm(ks[28], (DEPTH, N_EXPERTS, D, D_FF), D ** -0.5),
        "moe_w3": nrm(ks[29], (DEPTH, N_EXPERTS, D, D_FF), D ** -0.5),
        "moe_w2": nrm(ks[30], (DEPTH, N_EXPERTS, D_FF, D), D_FF ** -0.5),
    }


def reference(x, c, ctx, c_ctx, ada_w, ada_b, norm1_g, norm2_g, final_g, w_in,
              hy_conv_w, hy_conv_b, hy_fw1, hy_fb1, hy_fw2, hy_fb2, hy_fw3, hy_fb3,
              hy_fwout, hy_freq, hy_bias, hg_lb_raw, hg_norm_g, p_hy, p_hg, w_out,
              router_w, router_b, moe_w1, moe_w3, moe_w2):
    f32 = jnp.float32
    B, L, D = x.shape
    Lc = ctx.shape[1]
    lb = jnp.cumsum(jax.nn.softmax(hg_lb_raw.astype(f32), axis=0), axis=0)
    lb = lb - lb[:1]
    sc = jax.nn.silu(c.astype(f32))
    scc = jax.nn.silu(c_ctx.astype(f32))
    s_zero = jnp.zeros((B, HG_HEADS, HG_EXPAND, HG_DV), f32)
    q_sl = slice(OFF_HG, OFF_HG + D_HG)
    ifb_sl = slice(OFF_HG + D_HG, OFF_HG + 4 * D_HG)
    og_sl = slice(OFF_HG + 4 * D_HG, OFF_GATE)

    for layer in range(DEPTH):
        last = layer == DEPTH - 1
        aw = ada_w[layer].astype(f32)
        ab = ada_b[layer].astype(f32)
        mod = (sc @ aw + ab).reshape(B, 6, 1, D)
        n_mc = 2 if last else 6
        modc = (scc @ aw[:, :n_mc * D] + ab[:n_mc * D]).reshape(n_mc, 1, D)
        hyk = (hy_fw1[layer], hy_fb1[layer], hy_fw2[layer], hy_fb2[layer],
               hy_fw3[layer], hy_fb3[layer], hy_fwout[layer], hy_freq[layer])
        wl = w_in[layer]

        hx = _modulate(_rmsnorm(x, norm1_g[layer]), mod[:, 0], mod[:, 1])
        hc = _modulate(_rmsnorm(ctx, norm1_g[layer]), modc[0], modc[1])
        px = hx @ wl

        if last:
            vc, kgf_c, kgb_c = _hgrn2_kv(hc @ wl[:, ifb_sl], lb[layer])
            S_f = _final_state(kgf_c[0], vc, kgf_c[1])
            S_b = _final_state(_rev(kgb_c[0]), _rev(vc), _rev(kgb_c[1]))
        else:
            pc = hc @ wl
            qc = _heads(jax.nn.silu(pc[..., q_sl]))
            vc, kgf_c, kgb_c = _hgrn2_kv(pc[..., ifb_sl], lb[layer])
            oc, S_f, S_b = _bidir(qc, vc, kgf_c, kgb_c, s_zero, s_zero)
            y_hg_c = _hgrn2_readout(oc, pc[..., og_sl], hg_norm_g[layer])
            kf_c = _hyena_kernel_fft(Lc, *hyk)
            y_hy_c = _hyena(pc[..., :OFF_HG], hy_conv_w[layer], hy_conv_b[layer], kf_c, hy_bias[layer])
            mix_c = _merge(pc, y_hy_c, y_hg_c, p_hy[layer], p_hg[layer], w_out[layer])

        ql = _heads(jax.nn.silu(px[..., q_sl]))
        vl, kgf_l, kgb_l = _hgrn2_kv(px[..., ifb_sl], lb[layer])
        ol, _, _ = _bidir(ql, vl, kgf_l, kgb_l, S_f, S_b)
        y_hg_x = _hgrn2_readout(ol, px[..., og_sl], hg_norm_g[layer])
        kf_x = _hyena_kernel_fft(L, *hyk)
        y_hy_x = _hyena(px[..., :OFF_HG], hy_conv_w[layer], hy_conv_b[layer], kf_x, hy_bias[layer])
        mix_x = _merge(px, y_hy_x, y_hg_x, p_hy[layer], p_hg[layer], w_out[layer])
        x = x + (mod[:, 2] * mix_x.astype(f32)).astype(x.dtype)
        if not last:
            ctx = ctx + (modc[2] * mix_c.astype(f32)).astype(ctx.dtype)

        tok = _modulate(_rmsnorm(x, norm2_g[layer]), mod[:, 3], mod[:, 4]).reshape(B * L, D)
        if not last:
            h2c = _modulate(_rmsnorm(ctx, norm2_g[layer]), modc[3], modc[4]).reshape(B * Lc, D)
            tok = jnp.concatenate([h2c, tok], axis=0)
        y = _moe(tok, router_w, router_b, moe_w1[layer], moe_w3[layer], moe_w2[layer])
        if not last:
            ctx = ctx + (modc[5] * y[:B * Lc].reshape(B, Lc, D).astype(f32)).astype(ctx.dtype)
            y = y[B * Lc:]
        x = x + (mod[:, 5] * y.reshape(B, L, D).astype(f32)).astype(x.dtype)

    return _rmsnorm(x, final_g)
```

```python
import functools
import math

import jax
import jax.numpy as jnp
from jax import lax
from jax.experimental import pallas as pl
from jax.experimental.pallas import tpu as pltpu

f32 = jnp.float32
bf16 = jnp.bfloat16
i32 = jnp.int32

RMS_EPS = 1e-6
HEAD_DIM = 128
SCAN_CHUNK = 64
N_GROUPS = 4
HY_MAX_DECAY = math.log(1e-2) / 0.3
HY_MIN_DECAY = math.log(1e-2) / 1.5
FFT_N2 = 256
EMB_PAD = 64
MOE_ROWS = 256
VMEM_LIMIT = 52 * 1024 * 1024
HIGHEST = lax.Precision.HIGHEST

_NT = (((1,), (1,)), ((), ()))
_TN = (((0,), (0,)), ((), ()))


def _cp(sem, vmem=VMEM_LIMIT):
    return pltpu.CompilerParams(dimension_semantics=sem, vmem_limit_bytes=vmem)


def _silu(x):
    return x * jax.nn.sigmoid(x)


def _mod_kernel(sb_ref, w_ref, b_ref, o_ref, *, nrow, D, tn):
    rep = tn // 128

    def body(i, accs):
        k0 = pl.multiple_of(i * 8, 8)
        w = w_ref[pl.ds(k0, 8), :]
        out = []
        for m in range(nrow):
            sb = sb_ref[m, pl.ds(k0, 8), :]
            out.append(accs[m] + w * jnp.concatenate([sb] * rep, axis=1))
        return tuple(out)

    accs = lax.fori_loop(0, D // 8, body, tuple(jnp.zeros((8, tn), f32) for _ in range(nrow)), unroll=4)
    o_ref[...] = jnp.zeros_like(o_ref)
    for m in range(nrow):
        o_ref[m:m + 1, :] = jnp.sum(accs[m], axis=0, keepdims=True) + b_ref[...]


def _modulation(cond, ada_w, ada_b):
    nrow, D = cond.shape
    depth, _, n6 = ada_w.shape
    tn = 1024 if n6 % 1024 == 0 else n6
    sb = jnp.broadcast_to(_silu(cond)[:, :, None], (nrow, D, 128))
    return pl.pallas_call(
        functools.partial(_mod_kernel, nrow=nrow, D=D, tn=tn),
        out_shape=jax.ShapeDtypeStruct((depth, 8, n6), f32),
        grid=(depth, n6 // tn),
        in_specs=[
            pl.BlockSpec((nrow, D, 128), lambda l, j: (0, 0, 0)),
            pl.BlockSpec((None, D, tn), lambda l, j: (l, 0, j)),
            pl.BlockSpec((None, 1, tn), lambda l, j: (l, 0, j)),
        ],
        out_specs=pl.BlockSpec((None, 8, tn), lambda l, j: (l, 0, j)),
        compiler_params=_cp(("arbitrary", "arbitrary")),
        name="modulation",
    )(sb, ada_w, ada_b.reshape(depth, 1, n6))


def _inproj_kernel(seg_ref, x_ref, mod_ref, g_ref, w_ref, lb_ref, hy_ref, hg_ref, gt_ref, lg_ref, hx_scr, *, D):
    i = pl.program_id(0)
    j = pl.program_id(1)

    @pl.when(j == 0)
    def _():
        seg = seg_ref[i]
        xf = x_ref[...]
        ms = jnp.mean(xf * xf, axis=-1, keepdims=True)
        y = xf * lax.rsqrt(ms + RMS_EPS) * g_ref[...]
        shift = mod_ref[pl.ds(seg, 1), 0:D]
        scale = mod_ref[pl.ds(seg, 1), D:2 * D]
        hx_scr[...] = (y * (1.0 + scale) + shift).astype(bf16)

    acc = jnp.dot(hx_scr[...], w_ref[...], preferred_element_type=f32)

    @pl.when(j < 3)
    def _():
        hy_ref[...] = acc.astype(hy_ref.dtype)

    @pl.when((j == 3) | (j == 7))
    def _():
        hg_ref[...] = _silu(acc).astype(hg_ref.dtype)

    @pl.when(j == 4)
    def _():
        hg_ref[...] = acc.astype(hg_ref.dtype)

    @pl.when((j == 5) | (j == 6))
    def _():
        lb = lb_ref[pl.ds(j - 5, 1), :]
        f = lb + (1.0 - lb) * jax.nn.sigmoid(acc)
        hg_ref[...] = (1.0 - f).astype(hg_ref.dtype)
        lg_ref[...] = jnp.log(f)

    @pl.when(j >= 8)
    def _():
        gt_ref[...] = jax.nn.sigmoid(acc).astype(gt_ref.dtype)


def _inproj(xs, tile_seg, mod_l, g1, w_bf, lb_l, tm):
    T, D = xs.shape
    h = D // 2
    return pl.pallas_call(
        functools.partial(_inproj_kernel, D=D),
        out_shape=(
            jax.ShapeDtypeStruct((T, 3 * h), bf16),
            jax.ShapeDtypeStruct((T, 5 * h), bf16),
            jax.ShapeDtypeStruct((T, 2 * D), bf16),
            jax.ShapeDtypeStruct((T, 2 * h), f32),
        ),
        grid_spec=pltpu.PrefetchScalarGridSpec(
            num_scalar_prefetch=1,
            grid=(T // tm, 12),
            in_specs=[
                pl.BlockSpec((tm, D), lambda i, j, s: (i, 0)),
                pl.BlockSpec((8, 6 * D), lambda i, j, s: (0, 0)),
                pl.BlockSpec((1, D), lambda i, j, s: (0, 0)),
                pl.BlockSpec((D, h), lambda i, j, s: (0, j)),
                pl.BlockSpec((2, h), lambda i, j, s: (0, 0)),
            ],
            out_specs=[
                pl.BlockSpec((tm, h), lambda i, j, s: (i, jnp.minimum(j, 2))),
                pl.BlockSpec((tm, h), lambda i, j, s: (i, jnp.clip(j - 3, 0, 4))),
                pl.BlockSpec((tm, h), lambda i, j, s: (i, jnp.clip(j - 8, 0, 3))),
                pl.BlockSpec((tm, h), lambda i, j, s: (i, jnp.clip(j - 5, 0, 1))),
            ],
            scratch_shapes=[pltpu.VMEM((tm, D), bf16)],
        ),
        compiler_params=_cp(("arbitrary", "arbitrary")),
        name="inproj",
    )(tile_seg, xs, mod_l, g1, w_bf, lb_l)


def _split3(g):
    g1 = g.astype(bf16)
    r1 = g - g1.astype(f32)
    g2 = r1.astype(bf16)
    g3 = (r1 - g2.astype(f32)).astype(bf16)
    return g1, g2, g3


def _hgrn_kernel(q_ref, v_ref, k_ref, g_ref, o_ref, s_scr, *, rev, H):
    C = SCAN_CHUNK

    @pl.when(pl.program_id(1) == 0)
    def _():
        s_scr[...] = jnp.zeros_like(s_scr)

    g = g_ref[...]
    W = g.shape[1]
    ti = lax.broadcasted_iota(i32, (C, C), 0)
    si = lax.broadcasted_iota(i32, (C, C), 1)
    causal = (si >= ti) if rev else (si <= ti)
    tri = jnp.where(causal, 1.0, 0.0).astype(bf16)
    b = sum(jnp.dot(tri, gi, preferred_element_type=f32) for gi in _split3(g))

    def rowb(r, n):
        return jnp.broadcast_to(b[r:r + 1, :], (n, W))

    off = 1 if rev else 0
    refs = [
        rowb(31 + off, 64),
        jnp.concatenate([rowb(15 + off, 32), rowb(47 + off, 32)], axis=0),
        jnp.concatenate([rowb(16 * i + 7 + off, 16) for i in range(4)], axis=0),
    ]
    qf = q_ref[...].astype(f32)
    kf = k_ref[...].astype(f32)
    v = v_ref[...]
    qs, ks = [], []
    for m in refs:
        qs.append((qf * jnp.exp(jnp.minimum(b - m, 0.0))).astype(bf16))
        ks.append((kf * jnp.exp(jnp.minimum(m - b, 0.0))).astype(bf16))
    md = jnp.concatenate([rowb(8 * i + (7 if rev else 0), 8) for i in range(8)], axis=0)
    qs.append((qf * jnp.exp(jnp.minimum(b - md, 0.0))).astype(bf16))
    ks.append((kf * jnp.exp(jnp.minimum(md - b, 80.0))).astype(bf16))

    bend = b[0:1, :] if rev else b[C - 1:C, :]
    q_in = (qf * jnp.exp(b)).astype(bf16)
    k_st = (kf * jnp.exp(bend - b)).astype(bf16)
    dec = jnp.exp(bend)
    lvl = jnp.where((ti // 32) != (si // 32), 0,
                    jnp.where((ti // 16) != (si // 16), 1,
                              jnp.where((ti // 8) != (si // 8), 2, 3)))
    for h in range(H):
        hs = slice(h * HEAD_DIM, (h + 1) * HEAD_DIM)
        p = [lax.dot_general(qs[l][:, hs], ks[l][:, hs], _NT, preferred_element_type=f32) for l in range(4)]
        att = jnp.where(lvl == 0, p[0], jnp.where(lvl == 1, p[1], jnp.where(lvl == 2, p[2], p[3])))
        att = jnp.where(causal, att, 0.0).astype(bf16)
        st = s_scr[h]
        o_h = jnp.dot(att, v[:, hs], preferred_element_type=f32)
        o_h = o_h + lax.dot_general(q_in[:, hs], st.astype(bf16), _NT, preferred_element_type=f32)
        o_ref[:, hs] = o_h.astype(o_ref.dtype)
        s_scr[h] = st * dec[:, hs] + lax.dot_general(v[:, hs], k_st[:, hs], _TN, preferred_element_type=f32)


def _hgrn_scan(hg, lg, B, L, Lc, rev):
    T = hg.shape[0]
    h = lg.shape[1] // 2
    H = h // HEAD_DIM
    C = SCAN_CHUNK
    nL, nC = L // C, Lc // C

    def row(b, c):
        if rev:
            return jnp.where(c < nC, B * nL + b * nC + (nC - 1 - c), b * nL + (nL - 1 - (c - nC)))
        return jnp.where(c < nC, B * nL + b * nC + c, b * nL + (c - nC))

    kcol = 3 if rev else 2
    gcol = 1 if rev else 0
    return pl.pallas_call(
        functools.partial(_hgrn_kernel, rev=rev, H=H),
        out_shape=jax.ShapeDtypeStruct((T, h), bf16),
        grid=(B, nC + nL),
        in_specs=[
            pl.BlockSpec((C, h), lambda b, c: (row(b, c), 0)),
            pl.BlockSpec((C, h), lambda b, c: (row(b, c), 1)),
            pl.BlockSpec((C, h), lambda b, c: (row(b, c), kcol)),
            pl.BlockSpec((C, h), lambda b, c: (row(b, c), gcol)),
        ],
        out_specs=pl.BlockSpec((C, h), lambda b, c: (row(b, c), 0)),
        scratch_shapes=[pltpu.VMEM((H, HEAD_DIM, HEAD_DIM), f32)],
        compiler_params=_cp(("arbitrary", "arbitrary")),
        name="hgrn_bwd" if rev else "hgrn_fwd",
    )(hg, hg, hg, lg)


def _shortconv_kernel(p_ref, pv_ref, nx_ref, w_ref, b_ref, ul_ref, uc_ref, *, R, BL, L, Lc):
    i = pl.program_id(1)
    r0 = i * R
    lat = r0 < BL
    first = jnp.where(lat, (r0 % L) == 0, ((r0 - BL) % Lc) == 0)
    last = jnp.where(lat, ((r0 + R) % L) == 0, ((r0 + R - BL) % Lc) == 0)
    p = p_ref[...].astype(f32)
    prev_row = jnp.where(first, 0.0, pv_ref[15:16, :].astype(f32))
    next_row = jnp.where(last, 0.0, nx_ref[0:1, :].astype(f32))
    ri = lax.broadcasted_iota(i32, p.shape, 0)
    pm = jnp.where(ri == 0, prev_row, pltpu.roll(p, 1, 0))
    pp = jnp.where(ri == R - 1, next_row, pltpu.roll(p, R - 1, 0))
    u = pm * w_ref[0:1, :] + p * w_ref[1:2, :] + pp * w_ref[2:3, :] + b_ref[...]

    @pl.when(lat)
    def _():
        ul_ref[...] = u

    @pl.when(jnp.logical_not(lat))
    def _():
        uc_ref[...] = u


def _shortconv(p_hy, conv_w, conv_b, B, L, Lc):
    T, W3 = p_hy.shape
    h = W3 // 3
    BL, BLc = B * L, B * Lc
    R = min(256, Lc)
    nlat = BL // R
    return pl.pallas_call(
        functools.partial(_shortconv_kernel, R=R, BL=BL, L=L, Lc=Lc),
        out_shape=(jax.ShapeDtypeStruct((BL, W3), f32), jax.ShapeDtypeStruct((BLc, W3), f32)),
        grid=(3, T // R),
        in_specs=[
            pl.BlockSpec((R, h), lambda j, i: (i, j)),
            pl.BlockSpec((16, h), lambda j, i: (jnp.maximum(i * (R // 16) - 1, 0), j)),
            pl.BlockSpec((16, h), lambda j, i: (jnp.minimum((i + 1) * (R // 16), T // 16 - 1), j)),
            pl.BlockSpec((3, h), lambda j, i: (0, j)),
            pl.BlockSpec((1, h), lambda j, i: (0, j)),
        ],
        out_specs=[
            pl.BlockSpec((R, h), lambda j, i: (jnp.minimum(i, nlat - 1), j)),
            pl.BlockSpec((R, h), lambda j, i: (jnp.maximum(i - nlat, 0), j)),
        ],
        compiler_params=_cp(("arbitrary", "arbitrary")),
        name="shortconv",
    )(p_hy, p_hy, p_hy, conv_w, conv_b.reshape(1, W3))


def _filter_taps(z, fw1, fb1, fw2, fb2, fw3, fb3, wo_first, wo_second, freq, deltas, nfirst):
    h = jnp.sin(freq[0:1, :] * (jnp.dot(z, fw1, preferred_element_type=f32, precision=HIGHEST) + fb1))
    h = jnp.sin(freq[1:2, :] * (jnp.dot(h, fw2, preferred_element_type=f32, precision=HIGHEST) + fb2))
    h = jnp.sin(freq[2:3, :] * (jnp.dot(h, fw3, preferred_element_type=f32, precision=HIGHEST) + fb3))
    a = jnp.dot(h[:nfirst], wo_first, preferred_element_type=f32, precision=HIGHEST)
    b = jnp.dot(h[nfirst:], wo_second, preferred_element_type=f32, precision=HIGHEST)
    taps = jnp.concatenate([a, b], axis=0)
    return taps * jnp.exp(-z[:, 0:1] * deltas) * z[:, EMB_PAD - 1:EMB_PAD]


def _ctx_filter_kernel(z_ref, fw1, fb1, fw2, fb2, fw3, fb3, wo0, wo1, freq, dl, o_ref, *, Lc):
    taps = _filter_taps(z_ref[...], fw1[...], fb1[...], fw2[...], fb2[...], fw3[...], fb3[...],
                        wo1[...], wo0[...], freq[...], dl[...], Lc)
    o_ref[...] = taps / jnp.sum(jnp.abs(taps), axis=0, keepdims=True)


def _ctx_filter(ztab, fl, Lc, h):
    fw1, fb1, fw2, fb2, fw3, fb3, fwout, freq, deltas = fl
    cb = min(512, 2 * h)
    nb = (2 * h) // cb
    full = lambda a: pl.BlockSpec(a.shape, lambda j: (0,) * a.ndim)
    return pl.pallas_call(
        functools.partial(_ctx_filter_kernel, Lc=Lc),
        out_shape=jax.ShapeDtypeStruct((2 * Lc, 2 * h), f32),
        grid=(nb,),
        in_specs=[full(ztab), full(fw1), full(fb1), full(fw2), full(fb2), full(fw3), full(fb3),
                  pl.BlockSpec((fwout.shape[0], cb), lambda j: (0, j)),
                  pl.BlockSpec((fwout.shape[0], cb), lambda j: (0, nb + j)),
                  full(freq), pl.BlockSpec((1, cb), lambda j: (0, j))],
        out_specs=pl.BlockSpec((2 * Lc, cb), lambda j: (0, j)),
        compiler_params=_cp(("arbitrary",)),
        name="ctx_filter",
    )(ztab, fw1, fb1, fw2, fb2, fw3, fb3, fwout, fwout, freq, deltas)


def _ctx_conv_kernel(v_ref, x1_ref, x2_ref, k1_ref, k2_ref, b1_ref, b2_ref, o_ref, u_scr, *, Lc):
    def conv(kk_ref):
        def body(s, acc):
            return acc + kk_ref[pl.ds(Lc - s, Lc), :] * u_scr[pl.ds(s, 1), :]
        return lax.fori_loop(0, Lc, body, jnp.zeros((Lc, 128), f32))

    v = v_ref[...]
    u_scr[...] = v
    z = x1_ref[...] * (conv(k1_ref) + b1_ref[...] * v)
    u_scr[...] = z
    o_ref[...] = x2_ref[...] * (conv(k2_ref) + b2_ref[...] * z)


def _ctx_conv(u_ctx, kk, bias, B, Lc):
    h = u_ctx.shape[1] // 3
    nb = h // 128
    return pl.pallas_call(
        functools.partial(_ctx_conv_kernel, Lc=Lc),
        out_shape=jax.ShapeDtypeStruct((B * Lc, h), f32),
        grid=(B, nb),
        in_specs=[
            pl.BlockSpec((Lc, 128), lambda b, j: (b, j)),
            pl.BlockSpec((Lc, 128), lambda b, j: (b, nb + j)),
            pl.BlockSpec((Lc, 128), lambda b, j: (b, 2 * nb + j)),
            pl.BlockSpec((2 * Lc, 128), lambda b, j: (0, j)),
            pl.BlockSpec((2 * Lc, 128), lambda b, j: (0, nb + j)),
            pl.BlockSpec((1, 128), lambda b, j: (0, j)),
            pl.BlockSpec((1, 128), lambda b, j: (0, nb + j)),
        ],
        out_specs=pl.BlockSpec((Lc, 128), lambda b, j: (b, j)),
        scratch_shapes=[pltpu.VMEM((Lc, 128), f32)],
        compiler_params=_cp(("arbitrary", "arbitrary")),
        name="ctx_conv",
    )(u_ctx, u_ctx, u_ctx, kk, kk, bias.reshape(1, 2 * h), bias.reshape(1, 2 * h))


def _dft_constants(L):
    N2 = FFT_N2
    N1h = L // N2
    N1 = 2 * N1h
    N = N1 * N2
    k1 = jnp.arange(N1, dtype=i32)
    n1 = jnp.arange(N1, dtype=i32)
    n2 = jnp.arange(N2, dtype=i32)
    ph = (k1[None, :, None] * (n1[None, None, :] * N2 + n2[:, None, None])) % N
    ang = ph.astype(f32) * (2.0 * math.pi / N)
    gr, gi = jnp.cos(ang), -jnp.sin(ang)
    grh, gih = gr[:, :, :N1h], gi[:, :, :N1h]
    g_fwd = jnp.concatenate([jnp.concatenate([grh, -gih], axis=2),
                             jnp.concatenate([gih, grh], axis=2)], axis=1).astype(bf16)
    g_real = jnp.concatenate([gr, gi], axis=1).astype(bf16)
    mr = jnp.swapaxes(grh, 1, 2)
    mi = -jnp.swapaxes(gih, 1, 2)
    g_inv = jnp.concatenate([jnp.concatenate([mr, -mi], axis=2),
                             jnp.concatenate([mi, mr], axis=2)], axis=1).astype(bf16)
    kk = jnp.arange(N2, dtype=i32)
    a2 = ((kk[:, None] * kk[None, :]) % N2).astype(f32) * (2.0 * math.pi / N2)
    fr, fi = jnp.cos(a2), -jnp.sin(a2)
    fb_fwd = jnp.concatenate([jnp.concatenate([fr, -fi], axis=1),
                              jnp.concatenate([fi, fr], axis=1)], axis=0).astype(bf16)
    fb_inv = jnp.concatenate([jnp.concatenate([fr, fi], axis=1),
                              jnp.concatenate([-fi, fr], axis=1)], axis=0).astype(bf16)
    return dict(N1h=N1h, N1=N1, N2=N2, N=N, g_fwd=g_fwd, g_real=g_real, g_inv=g_inv, fb_fwd=fb_fwd, fb_inv=fb_inv)


def _pick(ref, j):
    return jnp.concatenate([ref[:, 0, j, :], ref[:, 1, j, :]], axis=0)


def _filt_a_kernel(z_ref, fw1, fb1, fw2, fb2, fw3, fb3, wo0, wo1, freq, dl, g_ref, y_ref, l1_ref, *, N1h):
    taps = _filter_taps(z_ref[...], fw1[...], fb1[...], fw2[...], fb2[...], fw3[...], fb3[...],
                        wo0[...], wo1[...], freq[...], dl[...], N1h)

    @pl.when(pl.program_id(0) == 0)
    def _():
        l1_ref[...] = jnp.zeros_like(l1_ref)

    l1_ref[...] += jnp.sum(jnp.abs(taps), axis=0, keepdims=True)
    y_ref[...] = jnp.dot(g_ref[...], taps.astype(bf16), preferred_element_type=f32)


def _filt_b_kernel(y_ref, fb_ref, l1_ref, k_ref, *, N):
    scale = 1.0 / (l1_ref[...] * float(N))
    for j in range(8):
        k_ref[j] = jnp.dot(fb_ref[...], _pick(y_ref, j).astype(bf16), preferred_element_type=f32) * scale


def _latent_filter(ztab, fl, dc, h):
    fw1, fb1, fw2, fb2, fw3, fb3, fwout, freq, deltas = fl
    N1h, N1, N2, N = dc["N1h"], dc["N1"], dc["N2"], dc["N"]
    full = lambda a: pl.BlockSpec(a.shape, lambda n: (0,) * a.ndim)
    yk, l1 = pl.pallas_call(
        functools.partial(_filt_a_kernel, N1h=N1h),
        out_shape=(jax.ShapeDtypeStruct((N2, 2 * N1, 2 * h), f32), jax.ShapeDtypeStruct((1, 2 * h), f32)),
        grid=(N2,),
        in_specs=[pl.BlockSpec((None, N1, EMB_PAD), lambda n: (n, 0, 0)),
                  full(fw1), full(fb1), full(fw2), full(fb2), full(fw3), full(fb3),
                  pl.BlockSpec((fwout.shape[0], 2 * h), lambda n: (0, 0)),
                  pl.BlockSpec((fwout.shape[0], 2 * h), lambda n: (0, 1)),
                  full(freq), full(deltas),
                  pl.BlockSpec((None, 2 * N1, N1), lambda n: (n, 0, 0))],
        out_specs=[pl.BlockSpec((None, 2 * N1, 2 * h), lambda n: (n, 0, 0)),
                   pl.BlockSpec((1, 2 * h), lambda n: (0, 0))],
        compiler_params=_cp(("arbitrary",)),
        name="filter_stage_a",
    )(ztab, fw1, fb1, fw2, fb2, fw3, fb3, fwout, fwout, freq, deltas, dc["g_real"])
    cb = min(256, 2 * h)
    kf = pl.pallas_call(
        functools.partial(_filt_b_kernel, N=N),
        out_shape=jax.ShapeDtypeStruct((N1, 2 * N2, 2 * h), f32),
        grid=(N1 // 8, (2 * h) // cb),
        in_specs=[pl.BlockSpec((N2, 2, 8, cb), lambda i, c: (0, 0, i, c)),
                  pl.BlockSpec((2 * N2, 2 * N2), lambda i, c: (0, 0)),
                  pl.BlockSpec((1, cb), lambda i, c: (0, c))],
        out_specs=pl.BlockSpec((8, 2 * N2, cb), lambda i, c: (i, 0, c)),
        compiler_params=_cp(("arbitrary", "arbitrary")),
        name="filter_stage_b",
    )(yk.reshape(N2, 2, N1, 2 * h), dc["fb_fwd"], l1)
    return kf


def _stack_batches(ref, j):
    return jnp.concatenate([ref[0, :, j, :], ref[1, :, j, :]], axis=0)


def _conv_a_kernel(u_ref, g_ref, y_ref):
    for j in range(8):
        y_ref[j] = jnp.dot(g_ref[j], _stack_batches(u_ref, j).astype(bf16), preferred_element_type=f32)


def _conv_b_kernel(y_ref, fbf_ref, fbi_ref, k_ref, w_ref, *, N2):
    for j in range(8):
        z = jnp.dot(fbf_ref[...], _pick(y_ref, j).astype(bf16), preferred_element_type=f32)
        zr, zi = z[:N2], z[N2:]
        kr, ki = k_ref[j, :N2, :], k_ref[j, N2:, :]
        p = jnp.concatenate([zr * kr - zi * ki, zr * ki + zi * kr], axis=0).astype(bf16)
        w_ref[j] = jnp.dot(fbi_ref[...], p, preferred_element_type=f32)


def _conv_mid_kernel(w_ref, gi_ref, g_ref, v_ref, x1_ref, b_ref, z_ref, y_ref):
    for j in range(8):
        y = jnp.dot(gi_ref[j], _pick(w_ref, j).astype(bf16), preferred_element_type=f32)
        vv = _stack_batches(v_ref, j)
        z = _stack_batches(x1_ref, j) * (y + b_ref[...] * vv)
        z_ref[j] = z
        y_ref[j] = jnp.dot(g_ref[j], z.astype(bf16), preferred_element_type=f32)


def _conv_out_kernel(w_ref, gi_ref, z_ref, x2_ref, b_ref, o_ref, *, N1h):
    for j in range(8):
        y = jnp.dot(gi_ref[j], _pick(w_ref, j).astype(bf16), preferred_element_type=f32)
        o = _stack_batches(x2_ref, j) * (y + b_ref[...] * z_ref[j])
        o_ref[0, :, j, :] = o[:N1h]
        o_ref[1, :, j, :] = o[N1h:]


def _latent_hyena(u_lat, kf, bias, dc, B, L):
    assert B == 2, "the long convolution packs exactly two batch rows into one complex sequence"
    h = u_lat.shape[1] // 3
    N1h, N1, N2 = dc["N1h"], dc["N1"], dc["N2"]
    u4 = u_lat.reshape(B, N1h, N2, 3 * h)
    cb = min(512, h)
    nb = h // cb
    cb2 = min(256, h)
    nb2 = h // cb2
    bias2 = bias.reshape(1, 2 * h)
    ublk = lambda col: pl.BlockSpec((B, N1h, 8, cb), lambda i, c: (0, 0, i, col * nb + c))

    ya = pl.pallas_call(
        _conv_a_kernel,
        out_shape=jax.ShapeDtypeStruct((N2, 2 * N1, h), f32),
        grid=(N2 // 8, nb),
        in_specs=[ublk(0), pl.BlockSpec((8, 2 * N1, 2 * N1h), lambda i, c: (i, 0, 0))],
        out_specs=pl.BlockSpec((8, 2 * N1, cb), lambda i, c: (i, 0, c)),
        compiler_params=_cp(("arbitrary", "arbitrary")),
        name="conv_stage_a",
    )(u4, dc["g_fwd"])

    def stage_b(y, order):
        return pl.pallas_call(
            functools.partial(_conv_b_kernel, N2=N2),
            out_shape=jax.ShapeDtypeStruct((N1, 2 * N2, h), f32),
            grid=(N1 // 8, nb2),
            in_specs=[pl.BlockSpec((N2, 2, 8, cb2), lambda i, c: (0, 0, i, c)),
                      pl.BlockSpec((2 * N2, 2 * N2), lambda i, c: (0, 0)),
                      pl.BlockSpec((2 * N2, 2 * N2), lambda i, c: (0, 0)),
                      pl.BlockSpec((8, 2 * N2, cb2), lambda i, c: (i, 0, order * nb2 + c))],
            out_specs=pl.BlockSpec((8, 2 * N2, cb2), lambda i, c: (i, 0, c)),
            compiler_params=_cp(("arbitrary", "arbitrary")),
            name="conv_stage_b",
        )(y.reshape(N2, 2, N1, h), dc["fb_fwd"], dc["fb_inv"], kf)

    w1 = stage_b(ya, 0)
    zp, ya2 = pl.pallas_call(
        _conv_mid_kernel,
        out_shape=(jax.ShapeDtypeStruct((N2, 2 * N1h, h), f32), jax.ShapeDtypeStruct((N2, 2 * N1, h), f32)),
        grid=(N2 // 8, nb),
        in_specs=[pl.BlockSpec((N1, 2, 8, cb), lambda i, c: (0, 0, i, c)),
                  pl.BlockSpec((8, 2 * N1h, 2 * N1), lambda i, c: (i, 0, 0)),
                  pl.BlockSpec((8, 2 * N1, 2 * N1h), lambda i, c: (i, 0, 0)),
                  ublk(0), ublk(1),
                  pl.BlockSpec((1, cb), lambda i, c: (0, c))],
        out_specs=[pl.BlockSpec((8, 2 * N1h, cb), lambda i, c: (i, 0, c)),
                   pl.BlockSpec((8, 2 * N1, cb), lambda i, c: (i, 0, c))],
        compiler_params=_cp(("arbitrary", "arbitrary")),
        name="conv_stage_mid",
    )(w1.reshape(N1, 2, N2, h), dc["g_inv"], dc["g_fwd"], u4, u4, bias2)
    w2 = stage_b(ya2, 1)
    y = pl.pallas_call(
        functools.partial(_conv_out_kernel, N1h=N1h),
        out_shape=jax.ShapeDtypeStruct((B, N1h, N2, h), f32),
        grid=(N2 // 8, nb),
        in_specs=[pl.BlockSpec((N1, 2, 8, cb), lambda i, c: (0, 0, i, c)),
                  pl.BlockSpec((8, 2 * N1h, 2 * N1), lambda i, c: (i, 0, 0)),
                  pl.BlockSpec((8, 2 * N1h, cb), lambda i, c: (i, 0, c)),
                  ublk(2),
                  pl.BlockSpec((1, cb), lambda i, c: (0, nb + c))],
        out_specs=pl.BlockSpec((B, N1h, 8, cb), lambda i, c: (0, 0, i, c)),
        compiler_params=_cp(("arbitrary", "arbitrary")),
        name="conv_stage_out",
    )(w2.reshape(N1, 2, N2, h), dc["g_inv"], zp, u4, bias2)
    return y.reshape(B * L, h)


def _merge_kernel(of_ref, ob_ref, og_ref, ng_ref, yl_ref, yc_ref, gt_hy_ref, gt_hg_ref, phy_ref, phg_ref, u_ref,
                  *, H, nlat):
    i = pl.program_id(0)
    o = of_ref[...].astype(f32) + ob_ref[...].astype(f32)
    og = og_ref[...].astype(f32)
    parts = []
    for h in range(H):
        hs = slice(h * HEAD_DIM, (h + 1) * HEAD_DIM)
        oh = o[:, hs]
        r = lax.rsqrt(jnp.mean(oh * oh, axis=-1, keepdims=True) + RMS_EPS)
        parts.append((oh * r * ng_ref[...] * og[:, hs]).astype(bf16))
    y_hg = jnp.concatenate(parts, axis=1)
    y_hy = jnp.where(i < nlat, yl_ref[...], yc_ref[...]).astype(bf16)
    a = jnp.dot(y_hy, phy_ref[...], preferred_element_type=f32)
    b = jnp.dot(y_hg, phg_ref[...], preferred_element_type=f32)
    u_ref[...] = (gt_hy_ref[...].astype(f32) * a + gt_hg_ref[...].astype(f32) * b).astype(u_ref.dtype)


def _merge(o_f, o_b, hg, ng, y_lat, y_ctx, gates, phy, phg, tm, ntile):
    h = o_f.shape[1]
    D = phy.shape[1]
    H = h // HEAD_DIM
    nlat = y_lat.shape[0] // tm
    nctx = y_ctx.shape[0] // tm
    return pl.pallas_call(
        functools.partial(_merge_kernel, H=H, nlat=nlat),
        out_shape=jax.ShapeDtypeStruct((ntile * tm, D), bf16),
        grid=(ntile,),
        in_specs=[
            pl.BlockSpec((tm, h), lambda i: (i, 0)),
            pl.BlockSpec((tm, h), lambda i: (i, 0)),
            pl.BlockSpec((tm, h), lambda i: (i, 4)),
            pl.BlockSpec((1, HEAD_DIM), lambda i: (0, 0)),
            pl.BlockSpec((tm, h), lambda i: (jnp.minimum(i, nlat - 1), 0)),
            pl.BlockSpec((tm, h), lambda i: (jnp.clip(i - nlat, 0, nctx - 1), 0)),
            pl.BlockSpec((tm, D), lambda i: (i, 0)),
            pl.BlockSpec((tm, D), lambda i: (i, 1)),
            pl.BlockSpec((h, D), lambda i: (0, 0)),
            pl.BlockSpec((h, D), lambda i: (0, 0)),
        ],
        out_specs=pl.BlockSpec((tm, D), lambda i: (i, 0)),
        compiler_params=_cp(("arbitrary",)),
        name="merge",
    )(o_f, o_b, hg, ng, y_lat, y_ctx, gates, gates, phy, phg)


def _norm_mod(x, g, shift, scale):
    ms = jnp.mean(x * x, axis=-1, keepdims=True)
    return (x * lax.rsqrt(ms + RMS_EPS) * g) * (1.0 + scale) + shift


def _outproj_kernel(seg_ref, u_ref, w_ref, x_ref, mod_ref, g2_ref, rh_ref, rl_ref, xo_ref, lo_ref, *, D):
    seg = seg_ref[pl.program_id(0)]
    mix = jnp.dot(u_ref[...], w_ref[...], preferred_element_type=f32)
    gate = mod_ref[pl.ds(seg, 1), 2 * D:3 * D]
    xn = x_ref[...] + gate * mix
    xo_ref[...] = xn
    t2 = _norm_mod(xn, g2_ref[...], mod_ref[pl.ds(seg, 1), 3 * D:4 * D], mod_ref[pl.ds(seg, 1), 4 * D:5 * D])
    th = t2.astype(bf16)
    tl = (t2 - th.astype(f32)).astype(bf16)
    lo_ref[...] = (lax.dot_general(rh_ref[...], th, _NT, preferred_element_type=f32)
                   + lax.dot_general(rh_ref[...], tl, _NT, preferred_element_type=f32)
                   + lax.dot_general(rl_ref[...], th, _NT, preferred_element_type=f32))


def _outproj(u, w_out, xs, tile_seg, mod_l, g2, rh, rl, tm, ntile):
    D = w_out.shape[0]
    E = rh.shape[0]
    return pl.pallas_call(
        functools.partial(_outproj_kernel, D=D),
        out_shape=(jax.ShapeDtypeStruct((ntile * tm, D), f32), jax.ShapeDtypeStruct((E, ntile * tm), f32)),
        grid_spec=pltpu.PrefetchScalarGridSpec(
            num_scalar_prefetch=1,
            grid=(ntile,),
            in_specs=[
                pl.BlockSpec((tm, D), lambda i, s: (i, 0)),
                pl.BlockSpec((D, D), lambda i, s: (0, 0)),
                pl.BlockSpec((tm, D), lambda i, s: (i, 0)),
                pl.BlockSpec((8, 6 * D), lambda i, s: (0, 0)),
                pl.BlockSpec((1, D), lambda i, s: (0, 0)),
                pl.BlockSpec((E, D), lambda i, s: (0, 0)),
                pl.BlockSpec((E, D), lambda i, s: (0, 0)),
            ],
            out_specs=[pl.BlockSpec((tm, D), lambda i, s: (i, 0)),
                       pl.BlockSpec((E, tm), lambda i, s: (0, i))],
        ),
        compiler_params=_cp(("arbitrary",)),
        name="outproj",
    )(tile_seg, u, w_out, xs, mod_l, g2, rh, rl)


_PAIRS = ((0, 1), (0, 2), (0, 3), (1, 2), (1, 3), (2, 3))


def _route_kernel(lo_ref, rb_ref, o_ref, *, E):
    per = E // N_GROUPS
    assert per == 4
    lo = lo_ref[...]
    sc = jax.nn.sigmoid(lo)
    sel = sc + rb_ref[...]
    srow = [sel[e:e + 1, :] for e in range(E)]
    crow = [sc[e:e + 1, :] for e in range(E)]
    gs = []
    for g in range(N_GROUPS):
        x = srow[per * g:per * g + per]
        m = x[0] + x[1]
        for (a, b) in _PAIRS[1:]:
            m = jnp.maximum(m, x[a] + x[b])
        gs.append(m)
    gbest = jnp.zeros_like(gs[0]).astype(i32)
    best = gs[0]
    for g in range(1, N_GROUPS):
        better = gs[g] > best
        gbest = jnp.where(better, g, gbest)
        best = jnp.where(better, gs[g], best)

    def pick(rows, i):
        out = rows[i]
        for g in range(1, N_GROUPS):
            out = jnp.where(gbest == g, rows[per * g + i], out)
        return out

    x = [pick(srow, i) for i in range(per)]
    s = [pick(crow, i) for i in range(per)]
    chosen = []
    for i in range(per):
        cnt = jnp.zeros_like(gbest)
        for j in range(per):
            if j == i:
                continue
            beats = (x[j] >= x[i]) if j < i else (x[j] > x[i])
            cnt = cnt + jnp.where(beats, 1, 0)
        chosen.append(cnt < 2)
    pair = jnp.zeros_like(gbest)
    wa = jnp.zeros_like(best)
    wb = jnp.zeros_like(best)
    for p, (a, b) in enumerate(_PAIRS):
        hit = jnp.where(chosen[a], jnp.where(chosen[b], 1, 0), 0) == 1
        pair = jnp.where(hit, p, pair)
        wa = jnp.where(hit, s[a], wa)
        wb = jnp.where(hit, s[b], wb)
    tot = wa + wb
    cls = (gbest * len(_PAIRS) + pair).astype(f32)
    o_ref[...] = jnp.concatenate([cls, wa / tot, wb / tot, jnp.zeros((5, cls.shape[1]), f32)], axis=0)


def _route(logits_t, router_b, tl):
    E, T = logits_t.shape
    return pl.pallas_call(
        functools.partial(_route_kernel, E=E),
        out_shape=jax.ShapeDtypeStruct((8, T), f32),
        grid=(T // tl,),
        in_specs=[pl.BlockSpec((E, tl), lambda i: (0, i)), pl.BlockSpec((E, 1), lambda i: (0, 0))],
        out_specs=pl.BlockSpec((8, tl), lambda i: (0, i)),
        compiler_params=_cp(("arbitrary",)),
        name="route",
    )(logits_t, router_b.reshape(E, 1))


def _moe_kernel(be_ref, nb_ref, x_ref, rw_ref, mod_ref, g2_ref, w1_ref, w3_ref, w2_ref, o_ref, h_scr, acc_scr, *, D, nseg):
    i = pl.program_id(0)
    k = pl.program_id(1)
    f = pl.program_id(2)
    nf = pl.num_programs(2)
    live = i < nb_ref[0]

    def seg_rows(col0):
        seg = rw_ref[:, 2:3]
        out = mod_ref[0:1, col0:col0 + D]
        for r in range(1, nseg):
            out = jnp.where(seg == float(r), mod_ref[r:r + 1, col0:col0 + D], out)
        return out

    @pl.when(live & (k == 0) & (f == 0))
    def _():
        h_scr[...] = _norm_mod(x_ref[...], g2_ref[...], seg_rows(3 * D), seg_rows(4 * D)).astype(bf16)
        acc_scr[...] = jnp.zeros_like(acc_scr)

    @pl.when(live)
    def _():
        hx = h_scr[...]
        a = jnp.dot(hx, w1_ref[...], preferred_element_type=f32)
        b = jnp.dot(hx, w3_ref[...], preferred_element_type=f32)
        wsel = jnp.where(k == 0, rw_ref[:, 0:1], rw_ref[:, 1:2])
        act = (_silu(a) * b * wsel).astype(bf16)
        acc_scr[...] += jnp.dot(act, w2_ref[...], preferred_element_type=f32)

    @pl.when(live & (k == 1) & (f == nf - 1))
    def _():
        o_ref[...] = x_ref[...] + seg_rows(5 * D) * acc_scr[...]

    @pl.when(jnp.logical_not(live) & (k == 1) & (f == nf - 1))
    def _():
        o_ref[...] = x_ref[...]


def _moe(xg, rw, blk_e, nblk_used, mod_l, g2, w1, w3, w2, nseg):
    P, D = xg.shape
    F = w1.shape[2]
    tf = min(512, F)
    nblk = P // MOE_ROWS
    return pl.pallas_call(
        functools.partial(_moe_kernel, D=D, nseg=nseg),
        out_shape=jax.ShapeDtypeStruct((P, D), f32),
        grid_spec=pltpu.PrefetchScalarGridSpec(
            num_scalar_prefetch=2,
            grid=(nblk, 2, F // tf),
            in_specs=[
                pl.BlockSpec((MOE_ROWS, D), lambda i, k, f, be, nb: (i, 0)),
                pl.BlockSpec((MOE_ROWS, 128), lambda i, k, f, be, nb: (i, 0)),
                pl.BlockSpec((8, 6 * D), lambda i, k, f, be, nb: (0, 0)),
                pl.BlockSpec((1, D), lambda i, k, f, be, nb: (0, 0)),
                pl.BlockSpec((None, D, tf), lambda i, k, f, be, nb: (be[2 * i + k], 0, f)),
                pl.BlockSpec((None, D, tf), lambda i, k, f, be, nb: (be[2 * i + k], 0, f)),
                pl.BlockSpec((None, tf, D), lambda i, k, f, be, nb: (be[2 * i + k], f, 0)),
            ],
            out_specs=pl.BlockSpec((MOE_ROWS, D), lambda i, k, f, be, nb: (i, 0)),
            scratch_shapes=[pltpu.VMEM((MOE_ROWS, D), bf16), pltpu.VMEM((MOE_ROWS, D), f32)],
        ),
        compiler_params=_cp(("arbitrary", "arbitrary", "arbitrary")),
        name="moe_experts",
    )(blk_e, nblk_used, xg, rw, mod_l, g2, w1, w3, w2)


def _final_norm_kernel(x_ref, g_ref, o_ref):
    x = x_ref[...]
    ms = jnp.mean(x * x, axis=-1, keepdims=True)
    o_ref[...] = x * lax.rsqrt(ms + RMS_EPS) * g_ref[...]


def _final_norm(x, g, tm):
    n, D = x.shape
    return pl.pallas_call(
        _final_norm_kernel,
        out_shape=jax.ShapeDtypeStruct((n, D), f32),
        grid=(n // tm,),
        in_specs=[pl.BlockSpec((tm, D), lambda i: (i, 0)), pl.BlockSpec((1, D), lambda i: (0, 0))],
        out_specs=pl.BlockSpec((tm, D), lambda i: (i, 0)),
        compiler_params=_cp(("arbitrary",)),
        name="final_norm",
    )(x, g)


def _feature_rows(pos, valid, Lx, bands):
    t = jnp.linspace(0.0, 1.0, Lx, dtype=f32)[jnp.clip(pos, 0, Lx - 1)][..., None]
    w = ((2.0 * math.pi / Lx) * jnp.clip(pos, 0, Lx - 1).astype(f32))[..., None]
    fb = jnp.linspace(1e-4, bands - 1, bands, dtype=f32)
    z = jnp.concatenate([t, jnp.cos(fb * w), -jnp.sin(fb * w)], axis=-1)
    z = jnp.pad(z, [(0, 0)] * (z.ndim - 1) + [(0, EMB_PAD - 1 - z.shape[-1])])
    return jnp.concatenate([z, valid.astype(f32)[..., None]], axis=-1)


def _latent_ztab(L, bands, dc):
    N1, N2 = dc["N1"], dc["N2"]
    tau = jnp.arange(N1, dtype=i32)[None, :] * N2 + jnp.arange(N2, dtype=i32)[:, None]
    pos = jnp.where(tau < L, tau, 2 * L - tau)
    return _feature_rows(pos, tau != L, L, bands)


def _ctx_ztab(Lc, bands):
    r = jnp.arange(2 * Lc, dtype=i32)
    return _feature_rows(jnp.abs(r - Lc), r != 0, Lc, bands)


def kernel(x, c, ctx, c_ctx, ada_w, ada_b, norm1_g, norm2_g, final_g, w_in, hy_conv_w, hy_conv_b, hy_fw1, hy_fb1,
           hy_fw2, hy_fb2, hy_fw3, hy_fb3, hy_fwout, hy_freq, hy_bias, hg_lb_raw, hg_norm_g, p_hy, p_hg, w_out,
           router_w, router_b, moe_w1, moe_w3, moe_w2):
    B, L, D = x.shape
    Lc = ctx.shape[1]
    depth = ada_w.shape[0]
    h = D // 2
    E = router_w.shape[1]
    emb = hy_fw1.shape[1]
    bands = (emb - 1) // 2
    BL, BLc = B * L, B * Lc
    T = BL + BLc
    tm = BLc
    assert L % tm == 0 and L % FFT_N2 == 0 and Lc % SCAN_CHUNK == 0 and h % HEAD_DIM == 0 and B + 1 <= 8
    nlat = BL // tm
    ntile = T // tm
    tile_seg = jnp.concatenate([jnp.repeat(jnp.arange(B, dtype=i32), L // tm), jnp.full((1,), B, i32)])

    xs = jnp.concatenate([x.reshape(BL, D), ctx.reshape(BLc, D)], axis=0).astype(f32)
    cond = jnp.concatenate([c.astype(f32), c_ctx.astype(f32)[None, :]], axis=0)
    mod = _modulation(cond, ada_w.astype(f32), ada_b.astype(f32))

    lb = jnp.cumsum(jax.nn.softmax(hg_lb_raw.astype(f32), axis=0), axis=0)
    lb = lb - lb[:1]

    dc = _dft_constants(L)
    zt_lat = _latent_ztab(L, bands, dc)
    zt_ctx = _ctx_ztab(Lc, bands)
    deltas = jnp.abs(jnp.linspace(HY_MIN_DECAY, HY_MAX_DECAY, 2 * h, dtype=f32)).reshape(1, 2 * h)

    rw_t = router_w.astype(f32).T
    rh = rw_t.astype(bf16)
    rl = (rw_t - rh.astype(f32)).astype(bf16)
    n_cls = N_GROUPS * len(_PAIRS)
    per = E // N_GROUPS
    cls_e = jnp.array([[per * g + a, per * g + b] for g in range(N_GROUPS) for (a, b) in _PAIRS], i32)

    for l in range(depth):
        last = l == depth - 1
        mod_l = mod[l]
        p_hyena, hg, gates, lg = _inproj(xs, tile_seg, mod_l, norm1_g[l].astype(f32).reshape(1, D),
                                         w_in[l].astype(bf16), lb[l], tm)
        o_f = _hgrn_scan(hg, lg, B, L, Lc, rev=False)
        o_b = _hgrn_scan(hg, lg, B, L, Lc, rev=True)

        pad = EMB_PAD - emb
        fl = (jnp.pad(hy_fw1[l].astype(f32), ((0, pad), (0, 0))), hy_fb1[l].astype(f32).reshape(1, -1),
              hy_fw2[l].astype(f32), hy_fb2[l].astype(f32).reshape(1, -1),
              hy_fw3[l].astype(f32), hy_fb3[l].astype(f32).reshape(1, -1),
              hy_fwout[l].astype(f32), hy_freq[l].astype(f32), deltas)
        u_lat, u_ctx = _shortconv(p_hyena, hy_conv_w[l].astype(f32), hy_conv_b[l].astype(f32), B, L, Lc)
        kf = _latent_filter(zt_lat, fl, dc, h)
        y_lat = _latent_hyena(u_lat, kf, hy_bias[l].astype(f32), dc, B, L)
        if last:
            y_ctx = jnp.zeros((BLc, h), f32)
            nt = nlat
        else:
            kk = _ctx_filter(zt_ctx, fl, Lc, h)
            y_ctx = _ctx_conv(u_ctx, kk, hy_bias[l].astype(f32), B, Lc)
            nt = ntile
        u = _merge(o_f, o_b, hg, hg_norm_g[l].astype(f32).reshape(1, HEAD_DIM), y_lat, y_ctx, gates,
                   p_hy[l].astype(bf16), p_hg[l].astype(bf16), tm, nt)
        g2 = norm2_g[l].astype(f32).reshape(1, D)
        x1, logits_t = _outproj(u, w_out[l].astype(bf16), xs, tile_seg, mod_l, g2, rh, rl, tm, nt)

        Tm = nt * tm
        route = _route(logits_t, router_b.astype(f32), tm)
        cls = route[0].astype(i32)
        counts = jnp.zeros((n_cls,), i32).at[cls].add(1)
        padded = (counts + MOE_ROWS - 1) // MOE_ROWS * MOE_ROWS
        pend = jnp.cumsum(padded)
        pstart = pend - padded
        sstart = jnp.cumsum(counts) - counts
        order = jnp.argsort(cls, stable=True).astype(i32)
        cls_s = cls[order]
        pos_s = pstart[cls_s] + jnp.arange(Tm, dtype=i32) - sstart[cls_s]
        P = (Tm + n_cls * (MOE_ROWS - 1) + MOE_ROWS - 1) // MOE_ROWS * MOE_ROWS
        nblk = P // MOE_ROWS
        src = jnp.zeros((P,), i32).at[pos_s].set(order)
        okf = jnp.zeros((P,), f32).at[pos_s].set(1.0)
        pos_of_tok = jnp.zeros((Tm,), i32).at[order].set(pos_s)
        blk_cls = jnp.minimum(jnp.searchsorted(pend, jnp.arange(nblk, dtype=i32) * MOE_ROWS, side="right"), n_cls - 1)
        blk_e = cls_e[blk_cls].reshape(-1).astype(i32)
        nblk_used = (pend[-1] // MOE_ROWS).astype(i32).reshape(1)
        seg_tok = jnp.repeat(tile_seg[:nt], tm).astype(f32)
        rw = jnp.stack([route[1][src] * okf, route[2][src] * okf, seg_tok[src]], axis=1)
        rw = jnp.pad(rw, ((0, 0), (0, 128 - 3)))
        xg = x1[src]
        yg = _moe(xg, rw, blk_e, nblk_used, mod_l, g2, moe_w1[l].astype(bf16), moe_w3[l].astype(bf16),
                  moe_w2[l].astype(bf16), B + 1)
        x2 = yg[pos_of_tok]
        if last:
            xs = x2
        else:
            xs = x2

    out = _final_norm(xs[:BL], final_g.astype(f32).reshape(1, D), tm)
    return out.reshape(B, L, D).astype(x.dtype)
```

```python
import functools
import math

import jax
import jax.numpy as jnp
from jax import lax
from jax.experimental import pallas as pl
from jax.experimental.pallas import tpu as pltpu

f32 = jnp.float32
bf16 = jnp.bfloat16
i32 = jnp.int32

RMS_EPS = 1e-6
HEAD_DIM = 128
SCAN_CHUNK = 64
N_GROUPS = 4
HY_MAX_DECAY = math.log(1e-2) / 0.3
HY_MIN_DECAY = math.log(1e-2) / 1.5
FFT_N2 = 256
EMB_PAD = 64
MOE_ROWS = 512
VMEM_LIMIT = 52 * 1024 * 1024
HIGHEST = lax.Precision.HIGHEST

_NT = (((1,), (1,)), ((), ()))
_TN = (((0,), (0,)), ((), ()))


def _cp(sem, vmem=VMEM_LIMIT):
    return pltpu.CompilerParams(dimension_semantics=sem, vmem_limit_bytes=vmem)


def _silu(x):
    return x * jax.nn.sigmoid(x)


def _mod_kernel(sb_ref, w_ref, b_ref, o_ref, *, nrow, D, tn):
    rep = tn // 128

    def body(i, accs):
        k0 = pl.multiple_of(i * 8, 8)
        w = w_ref[pl.ds(k0, 8), :]
        out = []
        for m in range(nrow):
            sb = sb_ref[m, pl.ds(k0, 8), :]
            out.append(accs[m] + w * jnp.concatenate([sb] * rep, axis=1))
        return tuple(out)

    accs = lax.fori_loop(0, D // 8, body, tuple(jnp.zeros((8, tn), f32) for _ in range(nrow)), unroll=4)
    o_ref[...] = jnp.zeros_like(o_ref)
    for m in range(nrow):
        o_ref[m:m + 1, :] = jnp.sum(accs[m], axis=0, keepdims=True) + b_ref[...]


def _modulation(cond, ada_w, ada_b):
    nrow, D = cond.shape
    depth, _, n6 = ada_w.shape
    tn = 1024 if n6 % 1024 == 0 else n6
    sb = jnp.broadcast_to(_silu(cond)[:, :, None], (nrow, D, 128))
    return pl.pallas_call(
        functools.partial(_mod_kernel, nrow=nrow, D=D, tn=tn),
        out_shape=jax.ShapeDtypeStruct((depth, 8, n6), f32),
        grid=(depth, n6 // tn),
        in_specs=[
            pl.BlockSpec((nrow, D, 128), lambda l, j: (0, 0, 0)),
            pl.BlockSpec((None, D, tn), lambda l, j: (l, 0, j)),
            pl.BlockSpec((None, 1, tn), lambda l, j: (l, 0, j)),
        ],
        out_specs=pl.BlockSpec((None, 8, tn), lambda l, j: (l, 0, j)),
        compiler_params=_cp(("arbitrary", "arbitrary")),
        name="modulation",
    )(sb, ada_w, ada_b.reshape(depth, 1, n6))


def _inproj_kernel(seg_ref, x_ref, mod_ref, g_ref, w_ref, lb_ref, p_ref, lg_ref, hx_scr, *, D):
    i = pl.program_id(0)
    j = pl.program_id(1)

    @pl.when(j == 0)
    def _():
        seg = seg_ref[i]
        xf = x_ref[...]
        ms = jnp.mean(xf * xf, axis=-1, keepdims=True)
        y = xf * lax.rsqrt(ms + RMS_EPS) * g_ref[...]
        shift = mod_ref[pl.ds(seg, 1), 0:D]
        scale = mod_ref[pl.ds(seg, 1), D:2 * D]
        hx_scr[...] = (y * (1.0 + scale) + shift).astype(bf16)

    acc = jnp.dot(hx_scr[...], w_ref[...], preferred_element_type=f32)

    @pl.when((j < 3) | (j == 4))
    def _():
        p_ref[...] = acc.astype(p_ref.dtype)

    @pl.when((j == 3) | (j == 7))
    def _():
        p_ref[...] = _silu(acc).astype(p_ref.dtype)

    @pl.when((j == 5) | (j == 6))
    def _():
        lb = lb_ref[pl.ds(j - 5, 1), :]
        f = lb + (1.0 - lb) * jax.nn.sigmoid(acc)
        p_ref[...] = (1.0 - f).astype(p_ref.dtype)
        lg_ref[...] = jnp.log(f)

    @pl.when(j >= 8)
    def _():
        p_ref[...] = jax.nn.sigmoid(acc).astype(p_ref.dtype)


COL_HY, COL_Q, COL_V, COL_KF, COL_KB, COL_OG, COL_GATE = 0, 3, 4, 5, 6, 7, 8


def _inproj(xs, tile_seg, mod_l, g1, w_bf, lb_l, tm):
    T, D = xs.shape
    h = D // 2
    return pl.pallas_call(
        functools.partial(_inproj_kernel, D=D),
        out_shape=(
            jax.ShapeDtypeStruct((T, 12 * h), bf16),
            jax.ShapeDtypeStruct((T, 2 * h), f32),
        ),
        grid_spec=pltpu.PrefetchScalarGridSpec(
            num_scalar_prefetch=1,
            grid=(T // tm, 12),
            in_specs=[
                pl.BlockSpec((tm, D), lambda i, j, s: (i, 0), pipeline_mode=pl.Buffered(1)),
                pl.BlockSpec((8, 6 * D), lambda i, j, s: (0, 0)),
                pl.BlockSpec((1, D), lambda i, j, s: (0, 0)),
                pl.BlockSpec((D, h), lambda i, j, s: (0, j)),
                pl.BlockSpec((2, h), lambda i, j, s: (0, 0)),
            ],
            out_specs=[
                pl.BlockSpec((tm, h), lambda i, j, s: (i, j)),
                pl.BlockSpec((tm, h), lambda i, j, s: (i, jnp.clip(j - COL_KF, 0, 1))),
            ],
            scratch_shapes=[pltpu.VMEM((tm, D), bf16)],
        ),
        compiler_params=_cp(("arbitrary", "arbitrary")),
        name="inproj",
    )(tile_seg, xs, mod_l, g1, w_bf, lb_l)


def _split3(g):
    g1 = g.astype(bf16)
    r1 = g - g1.astype(f32)
    g2 = r1.astype(bf16)
    g3 = (r1 - g2.astype(f32)).astype(bf16)
    return g1, g2, g3


def _hgrn_chunk(q_ref, v_ref, k_ref, g_ref, o_ref, s_scr, r0, d, *, rev, H):
    C = SCAN_CHUNK
    rows = slice(r0, r0 + C)
    g = g_ref[rows, :]
    W = g.shape[1]
    ti = lax.broadcasted_iota(i32, (C, C), 0)
    si = lax.broadcasted_iota(i32, (C, C), 1)
    causal = (si >= ti) if rev else (si <= ti)
    tri = jnp.where(causal, 1.0, 0.0).astype(bf16)
    b = sum(jnp.dot(tri, gi, preferred_element_type=f32) for gi in _split3(g))

    def rowb(r, n):
        return jnp.broadcast_to(b[r:r + 1, :], (n, W))

    off = 1 if rev else 0
    refs = [
        rowb(31 + off, 64),
        jnp.concatenate([rowb(15 + off, 32), rowb(47 + off, 32)], axis=0),
        jnp.concatenate([rowb(16 * i + 7 + off, 16) for i in range(4)], axis=0),
    ]
    qf = q_ref[rows, :].astype(f32)
    kf = k_ref[rows, :].astype(f32)
    v = v_ref[rows, :]
    qs, ks = [], []
    for m in refs:
        qs.append((qf * jnp.exp(jnp.minimum(b - m, 0.0))).astype(bf16))
        ks.append((kf * jnp.exp(jnp.minimum(m - b, 0.0))).astype(bf16))
    md = jnp.concatenate([rowb(8 * i + (7 if rev else 0), 8) for i in range(8)], axis=0)
    qs.append((qf * jnp.exp(jnp.minimum(b - md, 0.0))).astype(bf16))
    ks.append((kf * jnp.exp(jnp.minimum(md - b, 80.0))).astype(bf16))

    bend = b[0:1, :] if rev else b[C - 1:C, :]
    q_in = (qf * jnp.exp(b)).astype(bf16)
    k_st = (kf * jnp.exp(bend - b)).astype(bf16)
    dec = jnp.exp(bend)
    lvl = jnp.where((ti // 32) != (si // 32), 0,
                    jnp.where((ti // 16) != (si // 16), 1,
                              jnp.where((ti // 8) != (si // 8), 2, 3)))
    for h in range(H):
        hs = slice(h * HEAD_DIM, (h + 1) * HEAD_DIM)
        p = [lax.dot_general(qs[l][:, hs], ks[l][:, hs], _NT, preferred_element_type=f32) for l in range(4)]
        att = jnp.where(lvl == 0, p[0], jnp.where(lvl == 1, p[1], jnp.where(lvl == 2, p[2], p[3])))
        att = jnp.where(causal, att, 0.0).astype(bf16)
        st = s_scr[d, h]
        o_h = jnp.dot(att, v[:, hs], preferred_element_type=f32)
        o_h = o_h + lax.dot_general(q_in[:, hs], st.astype(bf16), _NT, preferred_element_type=f32)
        o_ref[rows, hs] = o_h.astype(o_ref.dtype)
        s_scr[d, h] = st * dec[:, hs] + lax.dot_general(v[:, hs], k_st[:, hs], _TN, preferred_element_type=f32)


def _hgrn_kernel(qf_ref, vf_ref, kf_ref, gf_ref, qb_ref, vb_ref, kb_ref, gb_ref, of_ref, ob_ref, s_scr, *, H, G):
    @pl.when(pl.program_id(1) == 0)
    def _():
        s_scr[...] = jnp.zeros_like(s_scr)

    for c in range(G):
        _hgrn_chunk(qf_ref, vf_ref, kf_ref, gf_ref, of_ref, s_scr, c * SCAN_CHUNK, 0, rev=False, H=H)
        _hgrn_chunk(qb_ref, vb_ref, kb_ref, gb_ref, ob_ref, s_scr, (G - 1 - c) * SCAN_CHUNK, 1, rev=True, H=H)


def _hgrn_scan(hg, lg, B, L, Lc):
    T = hg.shape[0]
    h = lg.shape[1] // 2
    H = h // HEAD_DIM
    G = min(4, Lc // SCAN_CHUNK)
    R = G * SCAN_CHUNK
    assert L % R == 0 and Lc % R == 0
    nL, nC = L // R, Lc // R

    def fwd(b, s):
        return jnp.where(s < nC, B * nL + b * nC + s, b * nL + (s - nC))

    def bwd(b, s):
        return jnp.where(s < nC, B * nL + b * nC + (nC - 1 - s), b * nL + (nL - 1 - (s - nC)))

    def spec(rowfn, col):
        return pl.BlockSpec((R, h), lambda b, s: (rowfn(b, s), col))

    return pl.pallas_call(
        functools.partial(_hgrn_kernel, H=H, G=G),
        out_shape=(jax.ShapeDtypeStruct((T, h), bf16), jax.ShapeDtypeStruct((T, h), bf16)),
        grid=(B, nC + nL),
        in_specs=[spec(fwd, COL_Q), spec(fwd, COL_V), spec(fwd, COL_KF), spec(fwd, 0),
                  spec(bwd, COL_Q), spec(bwd, COL_V), spec(bwd, COL_KB), spec(bwd, 1)],
        out_specs=[spec(fwd, 0), spec(bwd, 0)],
        scratch_shapes=[pltpu.VMEM((2, H, HEAD_DIM, HEAD_DIM), f32)],
        compiler_params=_cp(("arbitrary", "arbitrary")),
        name="hgrn_scan",
    )(hg, hg, hg, lg, hg, hg, hg, lg)


def _shortconv_kernel(p_ref, pv_ref, nx_ref, w_ref, b_ref, ul_ref, uc_ref, *, R, BL, L, Lc):
    i = pl.program_id(1)
    r0 = i * R
    lat = r0 < BL
    first = jnp.where(lat, (r0 % L) == 0, ((r0 - BL) % Lc) == 0)
    last = jnp.where(lat, ((r0 + R) % L) == 0, ((r0 + R - BL) % Lc) == 0)
    p = p_ref[...].astype(f32)
    prev_row = jnp.where(first, 0.0, pv_ref[15:16, :].astype(f32))
    next_row = jnp.where(last, 0.0, nx_ref[0:1, :].astype(f32))
    ri = lax.broadcasted_iota(i32, p.shape, 0)
    pm = jnp.where(ri == 0, prev_row, pltpu.roll(p, 1, 0))
    pp = jnp.where(ri == R - 1, next_row, pltpu.roll(p, R - 1, 0))
    u = pm * w_ref[0:1, :] + p * w_ref[1:2, :] + pp * w_ref[2:3, :] + b_ref[...]

    @pl.when(lat)
    def _():
        ul_ref[...] = u

    @pl.when(jnp.logical_not(lat))
    def _():
        uc_ref[...] = u


def _shortconv(p_hy, conv_w, conv_b, B, L, Lc):
    h = p_hy.shape[1] // 12
    W3 = 3 * h
    BL, BLc = B * L, B * Lc
    T = BL + BLc
    R = min(256, Lc)
    nlat = BL // R
    return pl.pallas_call(
        functools.partial(_shortconv_kernel, R=R, BL=BL, L=L, Lc=Lc),
        out_shape=(jax.ShapeDtypeStruct((BL, W3), f32), jax.ShapeDtypeStruct((BLc, W3), f32)),
        grid=(3, T // R),
        in_specs=[
            pl.BlockSpec((R, h), lambda j, i: (i, j)),
            pl.BlockSpec((16, h), lambda j, i: (jnp.maximum(i * (R // 16) - 1, 0), j)),
            pl.BlockSpec((16, h), lambda j, i: (jnp.minimum((i + 1) * (R // 16), T // 16 - 1), j)),
            pl.BlockSpec((3, h), lambda j, i: (0, j)),
            pl.BlockSpec((1, h), lambda j, i: (0, j)),
        ],
        out_specs=[
            pl.BlockSpec((R, h), lambda j, i: (jnp.minimum(i, nlat - 1), j)),
            pl.BlockSpec((R, h), lambda j, i: (jnp.maximum(i - nlat, 0), j)),
        ],
        compiler_params=_cp(("arbitrary", "arbitrary")),
        name="shortconv",
    )(p_hy, p_hy, p_hy, conv_w, conv_b.reshape(1, W3))


def _filter_taps(z, fw1, fb1, fw2, fb2, fw3, fb3, wo_first, wo_second, freq, deltas, nfirst):
    h = jnp.sin(freq[0:1, :] * (jnp.dot(z, fw1, preferred_element_type=f32, precision=HIGHEST) + fb1))
    h = jnp.sin(freq[1:2, :] * (jnp.dot(h, fw2, preferred_element_type=f32, precision=HIGHEST) + fb2))
    h = jnp.sin(freq[2:3, :] * (jnp.dot(h, fw3, preferred_element_type=f32, precision=HIGHEST) + fb3))
    hh = h.astype(bf16)
    hl = (h - hh.astype(f32)).astype(bf16)
    h3 = jnp.concatenate([hh, hl, hh], axis=1)
    a = jnp.dot(h3[:nfirst], wo_first, preferred_element_type=f32)
    b = jnp.dot(h3[nfirst:], wo_second, preferred_element_type=f32)
    taps = jnp.concatenate([a, b], axis=0)
    return taps * jnp.exp(-z[:, 0:1] * deltas) * z[:, EMB_PAD - 1:EMB_PAD]


def _ctx_filter_kernel(z_ref, fw1, fb1, fw2, fb2, fw3, fb3, wo0, wo1, freq, dl, o_ref, *, Lc):
    taps = _filter_taps(z_ref[...], fw1[...], fb1[...], fw2[...], fb2[...], fw3[...], fb3[...],
                        wo1[...], wo0[...], freq[...], dl[...], Lc)
    o_ref[...] = taps / jnp.sum(jnp.abs(taps), axis=0, keepdims=True)


def _ctx_filter(ztab, fl, Lc, h):
    fw1, fb1, fw2, fb2, fw3, fb3, fwout, freq, deltas = fl
    cb = min(512, 2 * h)
    nb = (2 * h) // cb
    full = lambda a: pl.BlockSpec(a.shape, lambda j: (0,) * a.ndim)
    return pl.pallas_call(
        functools.partial(_ctx_filter_kernel, Lc=Lc),
        out_shape=jax.ShapeDtypeStruct((2 * Lc, 2 * h), f32),
        grid=(nb,),
        in_specs=[full(ztab), full(fw1), full(fb1), full(fw2), full(fb2), full(fw3), full(fb3),
                  pl.BlockSpec((fwout.shape[0], cb), lambda j: (0, j)),
                  pl.BlockSpec((fwout.shape[0], cb), lambda j: (0, nb + j)),
                  full(freq), pl.BlockSpec((1, cb), lambda j: (0, j))],
        out_specs=pl.BlockSpec((2 * Lc, cb), lambda j: (0, j)),
        compiler_params=_cp(("arbitrary",)),
        name="ctx_filter",
    )(ztab, fw1, fb1, fw2, fb2, fw3, fb3, fwout, fwout, freq, deltas)


def _ctx_conv_kernel(v_ref, x1_ref, x2_ref, k1_ref, k2_ref, b1_ref, b2_ref, o_ref, u_scr, *, Lc):
    def conv(kk_ref):
        def body(s, acc):
            return acc + kk_ref[pl.ds(Lc - s, Lc), :] * u_scr[pl.ds(s, 1), :]
        return lax.fori_loop(0, Lc, body, jnp.zeros((Lc, 128), f32))

    v = v_ref[...]
    u_scr[...] = v
    z = x1_ref[...] * (conv(k1_ref) + b1_ref[...] * v)
    u_scr[...] = z
    o_ref[...] = x2_ref[...] * (conv(k2_ref) + b2_ref[...] * z)


def _ctx_conv(u_ctx, kk, bias, B, Lc):
    h = u_ctx.shape[1] // 3
    nb = h // 128
    return pl.pallas_call(
        functools.partial(_ctx_conv_kernel, Lc=Lc),
        out_shape=jax.ShapeDtypeStruct((B * Lc, h), f32),
        grid=(B, nb),
        in_specs=[
            pl.BlockSpec((Lc, 128), lambda b, j: (b, j)),
            pl.BlockSpec((Lc, 128), lambda b, j: (b, nb + j)),
            pl.BlockSpec((Lc, 128), lambda b, j: (b, 2 * nb + j)),
            pl.BlockSpec((2 * Lc, 128), lambda b, j: (0, j)),
            pl.BlockSpec((2 * Lc, 128), lambda b, j: (0, nb + j)),
            pl.BlockSpec((1, 128), lambda b, j: (0, j)),
            pl.BlockSpec((1, 128), lambda b, j: (0, nb + j)),
        ],
        out_specs=pl.BlockSpec((Lc, 128), lambda b, j: (b, j)),
        scratch_shapes=[pltpu.VMEM((Lc, 128), f32)],
        compiler_params=_cp(("arbitrary", "arbitrary")),
        name="ctx_conv",
    )(u_ctx, u_ctx, u_ctx, kk, kk, bias.reshape(1, 2 * h), bias.reshape(1, 2 * h))


def _dft_constants(L):
    N2 = FFT_N2
    N1h = L // N2
    N1 = 2 * N1h
    N = N1 * N2
    k1 = jnp.arange(N1, dtype=i32)
    n1 = jnp.arange(N1, dtype=i32)
    n2 = jnp.arange(N2, dtype=i32)
    ph = (k1[None, :, None] * (n1[None, None, :] * N2 + n2[:, None, None])) % N
    ang = ph.astype(f32) * (2.0 * math.pi / N)
    gr, gi = jnp.cos(ang), -jnp.sin(ang)
    grh, gih = gr[:, :, :N1h], gi[:, :, :N1h]
    g_fwd = jnp.concatenate([jnp.concatenate([grh, -gih], axis=2),
                             jnp.concatenate([gih, grh], axis=2)], axis=1).astype(bf16)
    g_real = jnp.concatenate([gr, gi], axis=1).astype(bf16)
    mr = jnp.swapaxes(grh, 1, 2)
    mi = -jnp.swapaxes(gih, 1, 2)
    g_inv = jnp.concatenate([jnp.concatenate([mr, -mi], axis=2),
                             jnp.concatenate([mi, mr], axis=2)], axis=1).astype(bf16)
    kk = jnp.arange(N2, dtype=i32)
    a2 = ((kk[:, None] * kk[None, :]) % N2).astype(f32) * (2.0 * math.pi / N2)
    fr, fi = jnp.cos(a2), -jnp.sin(a2)
    fb_fwd = jnp.concatenate([jnp.concatenate([fr, -fi], axis=1),
                              jnp.concatenate([fi, fr], axis=1)], axis=0).astype(bf16)
    fb_inv = jnp.concatenate([jnp.concatenate([fr, fi], axis=1),
                              jnp.concatenate([-fi, fr], axis=1)], axis=0).astype(bf16)
    return dict(N1h=N1h, N1=N1, N2=N2, N=N, g_fwd=g_fwd, g_real=g_real, g_inv=g_inv, fb_fwd=fb_fwd, fb_inv=fb_inv)


def _pick(ref, j):
    return jnp.concatenate([ref[:, 0, j, :], ref[:, 1, j, :]], axis=0)


def _filt_a_kernel(z_ref, fw1, fb1, fw2, fb2, fw3, fb3, wo0, wo1, freq, dl, g_ref, y_ref, l1_ref, *, N1h):
    taps = _filter_taps(z_ref[...], fw1[...], fb1[...], fw2[...], fb2[...], fw3[...], fb3[...],
                        wo0[...], wo1[...], freq[...], dl[...], N1h)

    @pl.when(pl.program_id(0) == 0)
    def _():
        l1_ref[...] = jnp.zeros_like(l1_ref)

    l1_ref[...] += jnp.sum(jnp.abs(taps), axis=0, keepdims=True)
    y_ref[...] = jnp.dot(g_ref[...], taps.astype(bf16), preferred_element_type=f32)


def _filt_b_kernel(y_ref, fb_ref, l1_ref, k_ref, *, N):
    scale = 1.0 / (l1_ref[...] * float(N))
    for j in range(8):
        k_ref[j] = jnp.dot(fb_ref[...], _pick(y_ref, j).astype(bf16), preferred_element_type=f32) * scale


def _latent_filter(ztab, fl, dc, h):
    fw1, fb1, fw2, fb2, fw3, fb3, fwout, freq, deltas = fl
    N1h, N1, N2, N = dc["N1h"], dc["N1"], dc["N2"], dc["N"]
    full = lambda a: pl.BlockSpec(a.shape, lambda n: (0,) * a.ndim)
    yk, l1 = pl.pallas_call(
        functools.partial(_filt_a_kernel, N1h=N1h),
        out_shape=(jax.ShapeDtypeStruct((N2, 2 * N1, 2 * h), f32), jax.ShapeDtypeStruct((1, 2 * h), f32)),
        grid=(N2,),
        in_specs=[pl.BlockSpec((None, N1, EMB_PAD), lambda n: (n, 0, 0)),
                  full(fw1), full(fb1), full(fw2), full(fb2), full(fw3), full(fb3),
                  pl.BlockSpec((fwout.shape[0], 2 * h), lambda n: (0, 0)),
                  pl.BlockSpec((fwout.shape[0], 2 * h), lambda n: (0, 1)),
                  full(freq), full(deltas),
                  pl.BlockSpec((None, 2 * N1, N1), lambda n: (n, 0, 0))],
        out_specs=[pl.BlockSpec((None, 2 * N1, 2 * h), lambda n: (n, 0, 0)),
                   pl.BlockSpec((1, 2 * h), lambda n: (0, 0))],
        compiler_params=_cp(("arbitrary",)),
        name="filter_stage_a",
    )(ztab, fw1, fb1, fw2, fb2, fw3, fb3, fwout, fwout, freq, deltas, dc["g_real"])
    cb = min(256, 2 * h)
    kf = pl.pallas_call(
        functools.partial(_filt_b_kernel, N=N),
        out_shape=jax.ShapeDtypeStruct((N1, 2 * N2, 2 * h), f32),
        grid=(N1 // 8, (2 * h) // cb),
        in_specs=[pl.BlockSpec((N2, 2, 8, cb), lambda i, c: (0, 0, i, c)),
                  pl.BlockSpec((2 * N2, 2 * N2), lambda i, c: (0, 0)),
                  pl.BlockSpec((1, cb), lambda i, c: (0, c))],
        out_specs=pl.BlockSpec((8, 2 * N2, cb), lambda i, c: (i, 0, c)),
        compiler_params=_cp(("arbitrary", "arbitrary")),
        name="filter_stage_b",
    )(yk.reshape(N2, 2, N1, 2 * h), dc["fb_fwd"], l1)
    return kf


def _stack_batches(ref, j):
    return jnp.concatenate([ref[0, :, j, :], ref[1, :, j, :]], axis=0)


def _conv_a_kernel(u_ref, g_ref, y_ref):
    for j in range(8):
        y_ref[j] = jnp.dot(g_ref[j], _stack_batches(u_ref, j).astype(bf16), preferred_element_type=f32)


def _conv_b_kernel(y_ref, fbf_ref, fbi_ref, k_ref, w_ref, *, N2):
    for j in range(8):
        z = jnp.dot(fbf_ref[...], _pick(y_ref, j).astype(bf16), preferred_element_type=f32)
        zr, zi = z[:N2], z[N2:]
        kr, ki = k_ref[j, :N2, :], k_ref[j, N2:, :]
        p = jnp.concatenate([zr * kr - zi * ki, zr * ki + zi * kr], axis=0).astype(bf16)
        w_ref[j] = jnp.dot(fbi_ref[...], p, preferred_element_type=f32)


def _conv_mid_kernel(w_ref, gi_ref, g_ref, v_ref, x1_ref, b_ref, z_ref, y_ref):
    for j in range(8):
        y = jnp.dot(gi_ref[j], _pick(w_ref, j).astype(bf16), preferred_element_type=f32)
        vv = _stack_batches(v_ref, j)
        z = _stack_batches(x1_ref, j) * (y + b_ref[...] * vv)
        z_ref[j] = z
        y_ref[j] = jnp.dot(g_ref[j], z.astype(bf16), preferred_element_type=f32)


def _conv_out_kernel(w_ref, gi_ref, z_ref, x2_ref, b_ref, o_ref, *, N1h):
    for j in range(8):
        y = jnp.dot(gi_ref[j], _pick(w_ref, j).astype(bf16), preferred_element_type=f32)
        o = _stack_batches(x2_ref, j) * (y + b_ref[...] * z_ref[j])
        o_ref[0, :, j, :] = o[:N1h]
        o_ref[1, :, j, :] = o[N1h:]


def _latent_hyena(u_lat, kf, bias, dc, B, L):
    assert B == 2, "the long convolution packs exactly two batch rows into one complex sequence"
    h = u_lat.shape[1] // 3
    N1h, N1, N2 = dc["N1h"], dc["N1"], dc["N2"]
    u4 = u_lat.reshape(B, N1h, N2, 3 * h)
    cb = min(512, h)
    nb = h // cb
    cb2 = min(256, h)
    nb2 = h // cb2
    bias2 = bias.reshape(1, 2 * h)
    ublk = lambda col: pl.BlockSpec((B, N1h, 8, cb), lambda i, c: (0, 0, i, col * nb + c))

    ya = pl.pallas_call(
        _conv_a_kernel,
        out_shape=jax.ShapeDtypeStruct((N2, 2 * N1, h), f32),
        grid=(N2 // 8, nb),
        in_specs=[ublk(0), pl.BlockSpec((8, 2 * N1, 2 * N1h), lambda i, c: (i, 0, 0))],
        out_specs=pl.BlockSpec((8, 2 * N1, cb), lambda i, c: (i, 0, c)),
        compiler_params=_cp(("arbitrary", "arbitrary")),
        name="conv_stage_a",
    )(u4, dc["g_fwd"])

    def stage_b(y, order):
        return pl.pallas_call(
            functools.partial(_conv_b_kernel, N2=N2),
            out_shape=jax.ShapeDtypeStruct((N1, 2 * N2, h), f32),
            grid=(N1 // 8, nb2),
            in_specs=[pl.BlockSpec((N2, 2, 8, cb2), lambda i, c: (0, 0, i, c)),
                      pl.BlockSpec((2 * N2, 2 * N2), lambda i, c: (0, 0)),
                      pl.BlockSpec((2 * N2, 2 * N2), lambda i, c: (0, 0)),
                      pl.BlockSpec((8, 2 * N2, cb2), lambda i, c: (i, 0, order * nb2 + c))],
            out_specs=pl.BlockSpec((8, 2 * N2, cb2), lambda i, c: (i, 0, c)),
            compiler_params=_cp(("arbitrary", "arbitrary")),
            name="conv_stage_b",
        )(y.reshape(N2, 2, N1, h), dc["fb_fwd"], dc["fb_inv"], kf)

    w1 = stage_b(ya, 0)
    zp, ya2 = pl.pallas_call(
        _conv_mid_kernel,
        out_shape=(jax.ShapeDtypeStruct((N2, 2 * N1h, h), f32), jax.ShapeDtypeStruct((N2, 2 * N1, h), f32)),
        grid=(N2 // 8, nb),
        in_specs=[pl.BlockSpec((N1, 2, 8, cb), lambda i, c: (0, 0, i, c)),
                  pl.BlockSpec((8, 2 * N1h, 2 * N1), lambda i, c: (i, 0, 0)),
                  pl.BlockSpec((8, 2 * N1, 2 * N1h), lambda i, c: (i, 0, 0)),
                  ublk(0), ublk(1),
                  pl.BlockSpec((1, cb), lambda i, c: (0, c))],
        out_specs=[pl.BlockSpec((8, 2 * N1h, cb), lambda i, c: (i, 0, c)),
                   pl.BlockSpec((8, 2 * N1, cb), lambda i, c: (i, 0, c))],
        compiler_params=_cp(("arbitrary", "arbitrary")),
        name="conv_stage_mid",
    )(w1.reshape(N1, 2, N2, h), dc["g_inv"], dc["g_fwd"], u4, u4, bias2)
    w2 = stage_b(ya2, 1)
    y = pl.pallas_call(
        functools.partial(_conv_out_kernel, N1h=N1h),
        out_shape=jax.ShapeDtypeStruct((B, N1h, N2, h), f32),
        grid=(N2 // 8, nb),
        in_specs=[pl.BlockSpec((N1, 2, 8, cb), lambda i, c: (0, 0, i, c)),
                  pl.BlockSpec((8, 2 * N1h, 2 * N1), lambda i, c: (i, 0, 0)),
                  pl.BlockSpec((8, 2 * N1h, cb), lambda i, c: (i, 0, c)),
                  ublk(2),
                  pl.BlockSpec((1, cb), lambda i, c: (0, nb + c))],
        out_specs=pl.BlockSpec((B, N1h, 8, cb), lambda i, c: (0, 0, i, c)),
        compiler_params=_cp(("arbitrary", "arbitrary")),
        name="conv_stage_out",
    )(w2.reshape(N1, 2, N2, h), dc["g_inv"], zp, u4, bias2)
    return y.reshape(B * L, h)


def _merge_kernel(of_ref, ob_ref, og_ref, ng_ref, yl_ref, yc_ref, gt_hy_ref, gt_hg_ref, phy_ref, phg_ref, u_ref,
                  *, H, nlat):
    i = pl.program_id(0)
    o = of_ref[...].astype(f32) + ob_ref[...].astype(f32)
    og = og_ref[...].astype(f32)
    parts = []
    for h in range(H):
        hs = slice(h * HEAD_DIM, (h + 1) * HEAD_DIM)
        oh = o[:, hs]
        r = lax.rsqrt(jnp.mean(oh * oh, axis=-1, keepdims=True) + RMS_EPS)
        parts.append((oh * r * ng_ref[...] * og[:, hs]).astype(bf16))
    y_hg = jnp.concatenate(parts, axis=1)
    y_hy = jnp.where(i < nlat, yl_ref[...], yc_ref[...]).astype(bf16)
    a = jnp.dot(y_hy, phy_ref[...], preferred_element_type=f32)
    b = jnp.dot(y_hg, phg_ref[...], preferred_element_type=f32)
    u_ref[...] = (gt_hy_ref[...].astype(f32) * a + gt_hg_ref[...].astype(f32) * b).astype(u_ref.dtype)


def _merge(o_f, o_b, hg, ng, y_lat, y_ctx, gates, phy, phg, tm, ntile):
    h = o_f.shape[1]
    D = phy.shape[1]
    H = h // HEAD_DIM
    nlat = y_lat.shape[0] // tm
    nctx = y_ctx.shape[0] // tm
    return pl.pallas_call(
        functools.partial(_merge_kernel, H=H, nlat=nlat),
        out_shape=jax.ShapeDtypeStruct((ntile * tm, D), bf16),
        grid=(ntile,),
        in_specs=[
            pl.BlockSpec((tm, h), lambda i: (i, 0)),
            pl.BlockSpec((tm, h), lambda i: (i, 0)),
            pl.BlockSpec((tm, h), lambda i: (i, COL_OG)),
            pl.BlockSpec((1, HEAD_DIM), lambda i: (0, 0)),
            pl.BlockSpec((tm, h), lambda i: (jnp.minimum(i, nlat - 1), 0)),
            pl.BlockSpec((tm, h), lambda i: (jnp.clip(i - nlat, 0, nctx - 1), 0)),
            pl.BlockSpec((tm, D), lambda i: (i, COL_GATE // 2)),
            pl.BlockSpec((tm, D), lambda i: (i, COL_GATE // 2 + 1)),
            pl.BlockSpec((h, D), lambda i: (0, 0)),
            pl.BlockSpec((h, D), lambda i: (0, 0)),
        ],
        out_specs=pl.BlockSpec((tm, D), lambda i: (i, 0)),
        compiler_params=_cp(("arbitrary",)),
        name="merge",
    )(o_f, o_b, hg, ng, y_lat, y_ctx, gates, gates, phy, phg)


def _norm_mod(x, g, shift, scale):
    ms = jnp.mean(x * x, axis=-1, keepdims=True)
    return (x * lax.rsqrt(ms + RMS_EPS) * g) * (1.0 + scale) + shift


def _outproj_kernel(seg_ref, u_ref, w_ref, x_ref, mod_ref, g2_ref, rh_ref, rl_ref, xo_ref, lo_ref, *, D):
    seg = seg_ref[pl.program_id(0)]
    mix = jnp.dot(u_ref[...], w_ref[...], preferred_element_type=f32)
    gate = mod_ref[pl.ds(seg, 1), 2 * D:3 * D]
    xn = x_ref[...] + gate * mix
    xo_ref[...] = xn
    t2 = _norm_mod(xn, g2_ref[...], mod_ref[pl.ds(seg, 1), 3 * D:4 * D], mod_ref[pl.ds(seg, 1), 4 * D:5 * D])
    th = t2.astype(bf16)
    tl = (t2 - th.astype(f32)).astype(bf16)
    lo_ref[...] = (lax.dot_general(rh_ref[...], th, _NT, preferred_element_type=f32)
                   + lax.dot_general(rh_ref[...], tl, _NT, preferred_element_type=f32)
                   + lax.dot_general(rl_ref[...], th, _NT, preferred_element_type=f32))


def _outproj(u, w_out, xs, tile_seg, mod_l, g2, rh, rl, tm, ntile):
    D = w_out.shape[0]
    E = rh.shape[0]
    return pl.pallas_call(
        functools.partial(_outproj_kernel, D=D),
        out_shape=(jax.ShapeDtypeStruct((ntile * tm, D), f32), jax.ShapeDtypeStruct((E, ntile * tm), f32)),
        grid_spec=pltpu.PrefetchScalarGridSpec(
            num_scalar_prefetch=1,
            grid=(ntile,),
            in_specs=[
                pl.BlockSpec((tm, D), lambda i, s: (i, 0)),
                pl.BlockSpec((D, D), lambda i, s: (0, 0)),
                pl.BlockSpec((tm, D), lambda i, s: (i, 0)),
                pl.BlockSpec((8, 6 * D), lambda i, s: (0, 0)),
                pl.BlockSpec((1, D), lambda i, s: (0, 0)),
                pl.BlockSpec((E, D), lambda i, s: (0, 0)),
                pl.BlockSpec((E, D), lambda i, s: (0, 0)),
            ],
            out_specs=[pl.BlockSpec((tm, D), lambda i, s: (i, 0)),
                       pl.BlockSpec((E, tm), lambda i, s: (0, i))],
        ),
        compiler_params=_cp(("arbitrary",)),
        name="outproj",
    )(tile_seg, u, w_out, xs, mod_l, g2, rh, rl)


_PAIRS = ((0, 1), (0, 2), (0, 3), (1, 2), (1, 3), (2, 3))


def _route_kernel(lo_ref, rb_ref, o_ref, *, E):
    per = E // N_GROUPS
    assert per == 4
    lo = lo_ref[...]
    sc = jax.nn.sigmoid(lo)
    sel = sc + rb_ref[...]
    srow = [sel[e:e + 1, :] for e in range(E)]
    crow = [sc[e:e + 1, :] for e in range(E)]
    gs = []
    for g in range(N_GROUPS):
        x = srow[per * g:per * g + per]
        m = x[0] + x[1]
        for (a, b) in _PAIRS[1:]:
            m = jnp.maximum(m, x[a] + x[b])
        gs.append(m)
    gbest = jnp.zeros_like(gs[0]).astype(i32)
    best = gs[0]
    for g in range(1, N_GROUPS):
        better = gs[g] > best
        gbest = jnp.where(better, g, gbest)
        best = jnp.where(better, gs[g], best)

    def pick(rows, i):
        out = rows[i]
        for g in range(1, N_GROUPS):
            out = jnp.where(gbest == g, rows[per * g + i], out)
        return out

    x = [pick(srow, i) for i in range(per)]
    s = [pick(crow, i) for i in range(per)]
    chosen = []
    for i in range(per):
        cnt = jnp.zeros_like(gbest)
        for j in range(per):
            if j == i:
                continue
            beats = (x[j] >= x[i]) if j < i else (x[j] > x[i])
            cnt = cnt + jnp.where(beats, 1, 0)
        chosen.append(cnt < 2)
    pair = jnp.zeros_like(gbest)
    wa = jnp.zeros_like(best)
    wb = jnp.zeros_like(best)
    for p, (a, b) in enumerate(_PAIRS):
        hit = jnp.where(chosen[a], jnp.where(chosen[b], 1, 0), 0) == 1
        pair = jnp.where(hit, p, pair)
        wa = jnp.where(hit, s[a], wa)
        wb = jnp.where(hit, s[b], wb)
    tot = wa + wb
    cls = (gbest * len(_PAIRS) + pair).astype(f32)
    o_ref[...] = jnp.concatenate([cls, wa / tot, wb / tot, jnp.zeros((5, cls.shape[1]), f32)], axis=0)


def _route(logits_t, router_b, tl):
    E, T = logits_t.shape
    return pl.pallas_call(
        functools.partial(_route_kernel, E=E),
        out_shape=jax.ShapeDtypeStruct((8, T), f32),
        grid=(T // tl,),
        in_specs=[pl.BlockSpec((E, tl), lambda i: (0, i)), pl.BlockSpec((E, 1), lambda i: (0, 0))],
        out_specs=pl.BlockSpec((8, tl), lambda i: (0, i)),
        compiler_params=_cp(("arbitrary",)),
        name="route",
    )(logits_t, router_b.reshape(E, 1))


def _moe_kernel(be_ref, nb_ref, x_ref, rw_ref, mod_ref, g2_ref, w1_ref, w3_ref, w2_ref, o_ref, h_scr, acc_scr, *, D, nseg):
    i = pl.program_id(0)
    k = pl.program_id(1)
    f = pl.program_id(2)
    nf = pl.num_programs(2)
    live = i < nb_ref[0]

    def seg_rows(col0):
        seg = rw_ref[:, 2:3]
        out = mod_ref[0:1, col0:col0 + D]
        for r in range(1, nseg):
            out = jnp.where(seg == float(r), mod_ref[r:r + 1, col0:col0 + D], out)
        return out

    @pl.when(live & (k == 0) & (f == 0))
    def _():
        h_scr[...] = _norm_mod(x_ref[...], g2_ref[...], seg_rows(3 * D), seg_rows(4 * D)).astype(bf16)
        acc_scr[...] = jnp.zeros_like(acc_scr)

    @pl.when(live)
    def _():
        hx = h_scr[...]
        a = jnp.dot(hx, w1_ref[...], preferred_element_type=f32)
        b = jnp.dot(hx, w3_ref[...], preferred_element_type=f32)
        wsel = jnp.where(k == 0, rw_ref[:, 0:1], rw_ref[:, 1:2])
        act = (_silu(a) * b * wsel).astype(bf16)
        acc_scr[...] += jnp.dot(act, w2_ref[...], preferred_element_type=f32)

    @pl.when(live & (k == 1) & (f == nf - 1))
    def _():
        o_ref[...] = x_ref[...] + seg_rows(5 * D) * acc_scr[...]

    @pl.when(jnp.logical_not(live) & (k == 1) & (f == nf - 1))
    def _():
        o_ref[...] = x_ref[...]


def _moe(xg, rw, blk_e, nblk_used, mod_l, g2, w1, w3, w2, nseg):
    P, D = xg.shape
    F = w1.shape[2]
    tf = min(512, F)
    nblk = P // MOE_ROWS
    nf = F // tf

    def wsel(i, k, f, be, nb):
        live = i < nb[0]
        e = be[2 * jnp.minimum(i, nb[0] - 1) + jnp.where(live, k, 1)]
        return e, jnp.where(live, f, nf - 1)

    def w13(i, k, f, be, nb):
        e, ff = wsel(i, k, f, be, nb)
        return (e, 0, ff)

    def w2m(i, k, f, be, nb):
        e, ff = wsel(i, k, f, be, nb)
        return (e, ff, 0)

    return pl.pallas_call(
        functools.partial(_moe_kernel, D=D, nseg=nseg),
        out_shape=jax.ShapeDtypeStruct((P, D), f32),
        grid_spec=pltpu.PrefetchScalarGridSpec(
            num_scalar_prefetch=2,
            grid=(nblk, 2, nf),
            in_specs=[
                pl.BlockSpec((MOE_ROWS, D), lambda i, k, f, be, nb: (i, 0)),
                pl.BlockSpec((MOE_ROWS, 128), lambda i, k, f, be, nb: (i, 0)),
                pl.BlockSpec((8, 6 * D), lambda i, k, f, be, nb: (0, 0)),
                pl.BlockSpec((1, D), lambda i, k, f, be, nb: (0, 0)),
                pl.BlockSpec((None, D, tf), w13),
                pl.BlockSpec((None, D, tf), w13),
                pl.BlockSpec((None, tf, D), w2m),
            ],
            out_specs=pl.BlockSpec((MOE_ROWS, D), lambda i, k, f, be, nb: (i, 0)),
            scratch_shapes=[pltpu.VMEM((MOE_ROWS, D), bf16), pltpu.VMEM((MOE_ROWS, D), f32)],
        ),
        compiler_params=_cp(("arbitrary", "arbitrary", "arbitrary")),
        name="moe_experts",
    )(blk_e, nblk_used, xg, rw, mod_l, g2, w1, w3, w2)


def _final_norm_kernel(x_ref, g_ref, o_ref):
    x = x_ref[...]
    ms = jnp.mean(x * x, axis=-1, keepdims=True)
    o_ref[...] = x * lax.rsqrt(ms + RMS_EPS) * g_ref[...]


def _final_norm(x, g, tm):
    n, D = x.shape
    return pl.pallas_call(
        _final_norm_kernel,
        out_shape=jax.ShapeDtypeStruct((n, D), f32),
        grid=(n // tm,),
        in_specs=[pl.BlockSpec((tm, D), lambda i: (i, 0)), pl.BlockSpec((1, D), lambda i: (0, 0))],
        out_specs=pl.BlockSpec((tm, D), lambda i: (i, 0)),
        compiler_params=_cp(("arbitrary",)),
        name="final_norm",
    )(x, g)


def _feature_rows(pos, valid, Lx, bands):
    t = jnp.linspace(0.0, 1.0, Lx, dtype=f32)[jnp.clip(pos, 0, Lx - 1)][..., None]
    w = ((2.0 * math.pi / Lx) * jnp.clip(pos, 0, Lx - 1).astype(f32))[..., None]
    fb = jnp.linspace(1e-4, bands - 1, bands, dtype=f32)
    z = jnp.concatenate([t, jnp.cos(fb * w), -jnp.sin(fb * w)], axis=-1)
    z = jnp.pad(z, [(0, 0)] * (z.ndim - 1) + [(0, EMB_PAD - 1 - z.shape[-1])])
    return jnp.concatenate([z, valid.astype(f32)[..., None]], axis=-1)


def _latent_ztab(L, bands, dc):
    N1, N2 = dc["N1"], dc["N2"]
    tau = jnp.arange(N1, dtype=i32)[None, :] * N2 + jnp.arange(N2, dtype=i32)[:, None]
    pos = jnp.where(tau < L, tau, 2 * L - tau)
    return _feature_rows(pos, tau != L, L, bands)


def _ctx_ztab(Lc, bands):
    r = jnp.arange(2 * Lc, dtype=i32)
    return _feature_rows(jnp.abs(r - Lc), r != 0, Lc, bands)


def kernel(x, c, ctx, c_ctx, ada_w, ada_b, norm1_g, norm2_g, final_g, w_in, hy_conv_w, hy_conv_b, hy_fw1, hy_fb1,
           hy_fw2, hy_fb2, hy_fw3, hy_fb3, hy_fwout, hy_freq, hy_bias, hg_lb_raw, hg_norm_g, p_hy, p_hg, w_out,
           router_w, router_b, moe_w1, moe_w3, moe_w2):
    B, L, D = x.shape
    Lc = ctx.shape[1]
    depth = ada_w.shape[0]
    h = D // 2
    E = router_w.shape[1]
    emb = hy_fw1.shape[1]
    bands = (emb - 1) // 2
    BL, BLc = B * L, B * Lc
    T = BL + BLc
    tm = BLc
    assert L % tm == 0 and L % FFT_N2 == 0 and Lc % SCAN_CHUNK == 0 and h % HEAD_DIM == 0 and B + 1 <= 8
    nlat = BL // tm
    ntile = T // tm
    tile_seg = jnp.concatenate([jnp.repeat(jnp.arange(B, dtype=i32), L // tm), jnp.full((1,), B, i32)])
    tmi = 2 * tm if L % (2 * tm) == 0 else tm
    T_pad = (T + tmi - 1) // tmi * tmi
    seg_in = jnp.minimum((jnp.arange(T_pad // tmi, dtype=i32) * tmi) // L, B)

    xs = jnp.concatenate([x.reshape(BL, D).astype(f32), ctx.reshape(BLc, D).astype(f32),
                          jnp.zeros((T_pad - T, D), f32)], axis=0)
    cond = jnp.concatenate([c.astype(f32), c_ctx.astype(f32)[None, :]], axis=0)
    mod = _modulation(cond, ada_w.astype(f32), ada_b.astype(f32))

    lb = jnp.cumsum(jax.nn.softmax(hg_lb_raw.astype(f32), axis=0), axis=0)
    lb = lb - lb[:1]

    dc = _dft_constants(L)
    zt_lat = _latent_ztab(L, bands, dc)
    zt_ctx = _ctx_ztab(Lc, bands)
    deltas = jnp.abs(jnp.linspace(HY_MIN_DECAY, HY_MAX_DECAY, 2 * h, dtype=f32)).reshape(1, 2 * h)

    rw_t = router_w.astype(f32).T
    rh = rw_t.astype(bf16)
    rl = (rw_t - rh.astype(f32)).astype(bf16)
    n_cls = N_GROUPS * len(_PAIRS)
    per = E // N_GROUPS
    cls_e = jnp.array([[per * g + a, per * g + b] for g in range(N_GROUPS) for (a, b) in _PAIRS], i32)

    for l in range(depth):
        last = l == depth - 1
        mod_l = mod[l]
        proj, lg = _inproj(xs, seg_in, mod_l, norm1_g[l].astype(f32).reshape(1, D), w_in[l].astype(bf16), lb[l], tmi)
        p_hyena = hg = gates = proj
        o_f, o_b = _hgrn_scan(hg, lg, B, L, Lc)

        pad = EMB_PAD - emb
        wo = hy_fwout[l].astype(f32)
        wo_hi = wo.astype(bf16)
        wo_lo = (wo - wo_hi.astype(f32)).astype(bf16)
        fl = (jnp.pad(hy_fw1[l].astype(f32), ((0, pad), (0, 0))), hy_fb1[l].astype(f32).reshape(1, -1),
              hy_fw2[l].astype(f32), hy_fb2[l].astype(f32).reshape(1, -1),
              hy_fw3[l].astype(f32), hy_fb3[l].astype(f32).reshape(1, -1),
              jnp.concatenate([wo_hi, wo_hi, wo_lo], axis=0), hy_freq[l].astype(f32), deltas)
        u_lat, u_ctx = _shortconv(p_hyena, hy_conv_w[l].astype(f32), hy_conv_b[l].astype(f32), B, L, Lc)
        kf = _latent_filter(zt_lat, fl, dc, h)
        y_lat = _latent_hyena(u_lat, kf, hy_bias[l].astype(f32), dc, B, L)
        if last:
            y_ctx = jnp.zeros((BLc, h), f32)
            nt = nlat
        else:
            kk = _ctx_filter(zt_ctx, fl, Lc, h)
            y_ctx = _ctx_conv(u_ctx, kk, hy_bias[l].astype(f32), B, Lc)
            nt = ntile
        u = _merge(o_f, o_b, hg, hg_norm_g[l].astype(f32).reshape(1, HEAD_DIM), y_lat, y_ctx, gates,
                   p_hy[l].astype(bf16), p_hg[l].astype(bf16), tm, nt)
        g2 = norm2_g[l].astype(f32).reshape(1, D)
        x1, logits_t = _outproj(u, w_out[l].astype(bf16), xs, tile_seg, mod_l, g2, rh, rl, tm, nt)

        Tm = nt * tm
        route = _route(logits_t, router_b.astype(f32), tm)
        cls = route[0].astype(i32)
        counts = jnp.sum((cls[:, None] == jnp.arange(n_cls, dtype=i32)[None, :]).astype(i32), axis=0)
        padded = (counts + MOE_ROWS - 1) // MOE_ROWS * MOE_ROWS
        pend = jnp.cumsum(padded)
        pstart = pend - padded
        sstart = jnp.cumsum(counts) - counts
        order = jnp.argsort(cls, stable=True).astype(i32)
        rank_of_tok = jnp.argsort(order).astype(i32)
        pos_of_tok = pstart[cls] + rank_of_tok - sstart[cls]
        P = (Tm + n_cls * (MOE_ROWS - 1) + MOE_ROWS - 1) // MOE_ROWS * MOE_ROWS
        nblk = P // MOE_ROWS
        blk_start = jnp.arange(nblk, dtype=i32) * MOE_ROWS
        blk_cls = jnp.minimum(jnp.sum((blk_start[:, None] >= pend[None, :]).astype(i32), axis=1), n_cls - 1)
        blk_e = cls_e[blk_cls].reshape(-1).astype(i32)
        nblk_used = (pend[-1] // MOE_ROWS).astype(i32).reshape(1)
        row_cls = jnp.repeat(blk_cls, MOE_ROWS)
        within = jnp.arange(P, dtype=i32) - pstart[row_cls]
        okf = (within < counts[row_cls]).astype(f32)
        src = order[jnp.clip(sstart[row_cls] + within, 0, Tm - 1)]
        seg_tok = jnp.repeat(tile_seg[:nt], tm).astype(f32)
        rw = jnp.stack([route[1][src] * okf, route[2][src] * okf, seg_tok[src]], axis=1)
        rw = jnp.pad(rw, ((0, 0), (0, 128 - 3)))
        xg = x1[src]
        yg = _moe(xg, rw, blk_e, nblk_used, mod_l, g2, moe_w1[l].astype(bf16), moe_w3[l].astype(bf16),
                  moe_w2[l].astype(bf16), B + 1)
        if last:
            xs = yg[pos_of_tok]
        else:
            xs = yg[jnp.concatenate([pos_of_tok, jnp.zeros((T_pad - T,), i32)])]

    out = _final_norm(xs[:BL], final_g.astype(f32).reshape(1, D), tm)
    return out.reshape(B, L, D).astype(x.dtype)
```

```python
import functools
import math

import jax
import jax.numpy as jnp
from jax import lax
from jax.experimental import pallas as pl
from jax.experimental.pallas import tpu as pltpu

f32 = jnp.float32
bf16 = jnp.bfloat16
i32 = jnp.int32

RMS_EPS = 1e-6
MXU_COLS = 256
LANES = 128
HEAD_DIM = 128
SCAN_CHUNK = 64
LOG2_E = 1.4426950408889634
DIAG_EXP2_CAP = 120.0
N_GROUPS = 4
HY_MAX_DECAY = math.log(1e-2) / 0.3
HY_MIN_DECAY = math.log(1e-2) / 1.5
FFT_N2 = 256
EMB_PAD = 64
MOE_ROWS = 512
VMEM_LIMIT = 52 * 1024 * 1024
HIGHEST = lax.Precision.HIGHEST

_NT = (((1,), (1,)), ((), ()))
_TN = (((0,), (0,)), ((), ()))


def _cp(sem, vmem=VMEM_LIMIT):
    return pltpu.CompilerParams(dimension_semantics=sem, vmem_limit_bytes=vmem)


def _silu(x):
    return x * jax.nn.sigmoid(x)


def _sigmoid_t(x):
    return 0.5 * jnp.tanh(0.5 * x) + 0.5


def _mod_kernel(sb_ref, w_ref, b_ref, o_ref, *, nrow, D, tn):
    rep = tn // 128

    def body(i, accs):
        k0 = pl.multiple_of(i * 8, 8)
        w = w_ref[pl.ds(k0, 8), :]
        out = []
        for m in range(nrow):
            sb = sb_ref[m, pl.ds(k0, 8), :]
            out.append(accs[m] + w * jnp.concatenate([sb] * rep, axis=1))
        return tuple(out)

    accs = lax.fori_loop(0, D // 8, body, tuple(jnp.zeros((8, tn), f32) for _ in range(nrow)), unroll=4)
    o_ref[...] = jnp.zeros_like(o_ref)
    for m in range(nrow):
        o_ref[m:m + 1, :] = jnp.sum(accs[m], axis=0, keepdims=True) + b_ref[...]


def _modulation(cond, ada_w, ada_b):
    nrow, D = cond.shape
    depth, _, n6 = ada_w.shape
    tn = 1024 if n6 % 1024 == 0 else n6
    sb = jnp.broadcast_to(_silu(cond)[:, :, None], (nrow, D, 128))
    return pl.pallas_call(
        functools.partial(_mod_kernel, nrow=nrow, D=D, tn=tn),
        out_shape=jax.ShapeDtypeStruct((depth, 8, n6), f32),
        grid=(depth, n6 // tn),
        in_specs=[
            pl.BlockSpec((nrow, D, 128), lambda l, j: (0, 0, 0)),
            pl.BlockSpec((None, D, tn), lambda l, j: (l, 0, j)),
            pl.BlockSpec((None, 1, tn), lambda l, j: (l, 0, j)),
        ],
        out_specs=pl.BlockSpec((None, 8, tn), lambda l, j: (l, 0, j)),
        compiler_params=_cp(("arbitrary", "arbitrary")),
        name="modulation",
    )(sb, ada_w, ada_b.reshape(depth, 1, n6))


def _inproj_kernel(seg_ref, x_ref, mod_ref, g_ref, w_ref, lb_ref, p_ref, lg_ref, hx_scr, *, D):
    i = pl.program_id(0)
    j = pl.program_id(1)

    @pl.when(j == 0)
    def _():
        seg = seg_ref[i]
        xf = x_ref[...]
        ms = jnp.mean(xf * xf, axis=-1, keepdims=True)
        y = xf * lax.rsqrt(ms + RMS_EPS) * g_ref[...]
        shift = mod_ref[pl.ds(seg, 1), 0:D]
        scale = mod_ref[pl.ds(seg, 1), D:2 * D]
        hx_scr[...] = (y * (1.0 + scale) + shift).astype(bf16)

    tn = w_ref.shape[1]
    pw = min(MXU_COLS, tn)

    def pieces(epilogue):
        for c in range(tn // pw):
            cs = slice(c * pw, (c + 1) * pw)
            epilogue(cs, jnp.dot(hx_scr[...], w_ref[:, cs], preferred_element_type=f32))

    @pl.when((j < 3) | (j == 4))
    def _():
        def epi(cs, acc):
            p_ref[:, cs] = acc.astype(p_ref.dtype)
        pieces(epi)

    @pl.when((j == 3) | (j == 7))
    def _():
        def epi(cs, acc):
            p_ref[:, cs] = (acc * _sigmoid_t(acc)).astype(p_ref.dtype)
        pieces(epi)

    @pl.when((j == 5) | (j == 6))
    def _():
        def epi(cs, acc):
            lb = lb_ref[pl.ds(j - 5, 1), cs]
            f = lb + (1.0 - lb) * jax.nn.sigmoid(acc)
            p_ref[:, cs] = (1.0 - f).astype(p_ref.dtype)
            lg_ref[:, cs] = jnp.log(f)
        pieces(epi)

    @pl.when(j >= 8)
    def _():
        def epi(cs, acc):
            p_ref[:, cs] = _sigmoid_t(acc).astype(p_ref.dtype)
        pieces(epi)


COL_HY, COL_Q, COL_V, COL_KF, COL_KB, COL_OG, COL_GATE = 0, 3, 4, 5, 6, 7, 8


def _inproj(xs, tile_seg, mod_l, g1, w_bf, lb_l, tm):
    T, D = xs.shape
    h = D // 2
    return pl.pallas_call(
        functools.partial(_inproj_kernel, D=D),
        out_shape=(
            jax.ShapeDtypeStruct((T, 12 * h), bf16),
            jax.ShapeDtypeStruct((T, 2 * h), f32),
        ),
        grid_spec=pltpu.PrefetchScalarGridSpec(
            num_scalar_prefetch=1,
            grid=(T // tm, 12),
            in_specs=[
                pl.BlockSpec((tm, D), lambda i, j, s: (i, 0), pipeline_mode=pl.Buffered(1)),
                pl.BlockSpec((8, 6 * D), lambda i, j, s: (0, 0)),
                pl.BlockSpec((1, D), lambda i, j, s: (0, 0)),
                pl.BlockSpec((D, h), lambda i, j, s: (0, j)),
                pl.BlockSpec((2, h), lambda i, j, s: (0, 0)),
            ],
            out_specs=[
                pl.BlockSpec((tm, h), lambda i, j, s: (i, j)),
                pl.BlockSpec((tm, h), lambda i, j, s: (i, jnp.clip(j - COL_KF, 0, 1))),
            ],
            scratch_shapes=[pltpu.VMEM((tm, D), bf16)],
        ),
        compiler_params=_cp(("arbitrary", "arbitrary")),
        name="inproj",
    )(tile_seg, xs, mod_l, g1, w_bf, lb_l)


def _split3(g):
    g1 = g.astype(bf16)
    r1 = g - g1.astype(f32)
    g2 = r1.astype(bf16)
    g3 = (r1 - g2.astype(f32)).astype(bf16)
    return g1, g2, g3


def _hgrn_chunk(q_ref, v_ref, k_ref, g_ref, o_ref, s_scr, r0, d, *, rev, H):
    C = SCAN_CHUNK
    rows = slice(r0, r0 + C)
    g = g_ref[rows, :]
    W = g.shape[1]
    ti = lax.broadcasted_iota(i32, (C, C), 0)
    si = lax.broadcasted_iota(i32, (C, C), 1)
    causal = (si >= ti) if rev else (si <= ti)
    tri = jnp.where(causal, 1.0, 0.0).astype(bf16)
    b = sum(jnp.dot(tri, gi, preferred_element_type=f32) for gi in _split3(g * LOG2_E))

    def rowb(r, n):
        return jnp.broadcast_to(b[r:r + 1, :], (n, W))

    off = 1 if rev else 0
    refs = [
        rowb(31 + off, 64),
        jnp.concatenate([rowb(15 + off, 32), rowb(47 + off, 32)], axis=0),
        jnp.concatenate([rowb(16 * i + 7 + off, 16) for i in range(4)], axis=0),
    ]
    qb = q_ref[rows, :]
    kb = k_ref[rows, :]
    v = v_ref[rows, :]
    qs, ks = [], []
    for m in refs:
        qs.append(qb * jnp.exp2(b - m).astype(bf16))
        ks.append(kb * jnp.exp2(m - b).astype(bf16))
    md = jnp.concatenate([rowb(8 * i + (7 if rev else 0), 8) for i in range(8)], axis=0)
    qs.append(qb * jnp.exp2(b - md).astype(bf16))
    ks.append(kb * jnp.exp2(jnp.minimum(md - b, DIAG_EXP2_CAP)).astype(bf16))

    bend = b[0:1, :] if rev else b[C - 1:C, :]
    q_in = qb * jnp.exp2(b).astype(bf16)
    k_st = kb * jnp.exp2(bend - b).astype(bf16)
    dec = jnp.exp2(bend)
    lvl = jnp.where((ti // 32) != (si // 32), 0,
                    jnp.where((ti // 16) != (si // 16), 1,
                              jnp.where((ti // 8) != (si // 8), 2, 3)))
    for h in range(H):
        hs = slice(h * HEAD_DIM, (h + 1) * HEAD_DIM)
        p = [lax.dot_general(qs[l][:, hs], ks[l][:, hs], _NT, preferred_element_type=f32) for l in range(4)]
        att = jnp.where(lvl == 0, p[0], jnp.where(lvl == 1, p[1], jnp.where(lvl == 2, p[2], p[3])))
        att = jnp.where(causal, att, 0.0).astype(bf16)
        st = s_scr[d, h]
        o_h = jnp.dot(att, v[:, hs], preferred_element_type=f32)
        o_h = o_h + lax.dot_general(q_in[:, hs], st.astype(bf16), _NT, preferred_element_type=f32)
        o_ref[rows, hs] = o_h.astype(o_ref.dtype)
        s_scr[d, h] = st * dec[:, hs] + lax.dot_general(v[:, hs], k_st[:, hs], _TN, preferred_element_type=f32)


def _hgrn_kernel(qf_ref, vf_ref, kf_ref, gf_ref, qb_ref, vb_ref, kb_ref, gb_ref, of_ref, ob_ref, s_scr, *, H, G):
    @pl.when(pl.program_id(1) == 0)
    def _():
        s_scr[...] = jnp.zeros_like(s_scr)

    for c in range(G):
        _hgrn_chunk(qf_ref, vf_ref, kf_ref, gf_ref, of_ref, s_scr, c * SCAN_CHUNK, 0, rev=False, H=H)
        _hgrn_chunk(qb_ref, vb_ref, kb_ref, gb_ref, ob_ref, s_scr, (G - 1 - c) * SCAN_CHUNK, 1, rev=True, H=H)


def _hgrn_scan(hg, lg, B, L, Lc):
    T = hg.shape[0]
    h = lg.shape[1] // 2
    H = h // HEAD_DIM
    G = min(4, Lc // SCAN_CHUNK)
    R = G * SCAN_CHUNK
    assert L % R == 0 and Lc % R == 0
    nL, nC = L // R, Lc // R

    def fwd(b, s):
        return jnp.where(s < nC, B * nL + b * nC + s, b * nL + (s - nC))

    def bwd(b, s):
        return jnp.where(s < nC, B * nL + b * nC + (nC - 1 - s), b * nL + (nL - 1 - (s - nC)))

    def spec(rowfn, col):
        return pl.BlockSpec((R, h), lambda b, s: (rowfn(b, s), col))

    return pl.pallas_call(
        functools.partial(_hgrn_kernel, H=H, G=G),
        out_shape=(jax.ShapeDtypeStruct((T, h), bf16), jax.ShapeDtypeStruct((T, h), bf16)),
        grid=(B, nC + nL),
        in_specs=[spec(fwd, COL_Q), spec(fwd, COL_V), spec(fwd, COL_KF), spec(fwd, 0),
                  spec(bwd, COL_Q), spec(bwd, COL_V), spec(bwd, COL_KB), spec(bwd, 1)],
        out_specs=[spec(fwd, 0), spec(bwd, 0)],
        scratch_shapes=[pltpu.VMEM((2, H, HEAD_DIM, HEAD_DIM), f32)],
        compiler_params=_cp(("arbitrary", "arbitrary")),
        name="hgrn_scan",
    )(hg, hg, hg, lg, hg, hg, hg, lg)


def _shortconv_kernel(p_ref, pv_ref, nx_ref, w_ref, b_ref, ul_ref, uc_ref, *, R, BL, L, Lc):
    i = pl.program_id(1)
    r0 = i * R
    lat = r0 < BL
    first = jnp.where(lat, (r0 % L) == 0, ((r0 - BL) % Lc) == 0)
    last = jnp.where(lat, ((r0 + R) % L) == 0, ((r0 + R - BL) % Lc) == 0)
    ri = lax.broadcasted_iota(i32, (R, LANES), 0)

    def conv_cols(o_ref):
        for c in range(p_ref.shape[1] // LANES):
            cs = slice(c * LANES, (c + 1) * LANES)
            p = p_ref[:, cs].astype(f32)
            prev_row = jnp.where(first, 0.0, pv_ref[15:16, cs].astype(f32))
            next_row = jnp.where(last, 0.0, nx_ref[0:1, cs].astype(f32))
            pm = jnp.where(ri == 0, prev_row, pltpu.roll(p, 1, 0))
            pp = jnp.where(ri == R - 1, next_row, pltpu.roll(p, R - 1, 0))
            o_ref[:, cs] = pm * w_ref[0:1, cs] + p * w_ref[1:2, cs] + pp * w_ref[2:3, cs] + b_ref[:, cs]

    @pl.when(lat)
    def _():
        conv_cols(ul_ref)

    @pl.when(jnp.logical_not(lat))
    def _():
        conv_cols(uc_ref)


def _shortconv(p_hy, conv_w, conv_b, B, L, Lc):
    h = p_hy.shape[1] // 12
    W3 = 3 * h
    BL, BLc = B * L, B * Lc
    T = BL + BLc
    R = min(256, Lc)
    nlat = BL // R
    return pl.pallas_call(
        functools.partial(_shortconv_kernel, R=R, BL=BL, L=L, Lc=Lc),
        out_shape=(jax.ShapeDtypeStruct((BL, W3), f32), jax.ShapeDtypeStruct((BLc, W3), f32)),
        grid=(3, T // R),
        in_specs=[
            pl.BlockSpec((R, h), lambda j, i: (i, j)),
            pl.BlockSpec((16, h), lambda j, i: (jnp.maximum(i * (R // 16) - 1, 0), j)),
            pl.BlockSpec((16, h), lambda j, i: (jnp.minimum((i + 1) * (R // 16), T // 16 - 1), j)),
            pl.BlockSpec((3, h), lambda j, i: (0, j)),
            pl.BlockSpec((1, h), lambda j, i: (0, j)),
        ],
        out_specs=[
            pl.BlockSpec((R, h), lambda j, i: (jnp.minimum(i, nlat - 1), j)),
            pl.BlockSpec((R, h), lambda j, i: (jnp.maximum(i - nlat, 0), j)),
        ],
        compiler_params=_cp(("arbitrary", "arbitrary")),
        name="shortconv",
    )(p_hy, p_hy, p_hy, conv_w, conv_b.reshape(1, W3))


def _filter_taps(z, fw1, fb1, fw2, fb2, fw3, fb3, wo_first, wo_second, freq, deltas, nfirst):
    h = jnp.sin(freq[0:1, :] * (jnp.dot(z, fw1, preferred_element_type=f32, precision=HIGHEST) + fb1))
    h = jnp.sin(freq[1:2, :] * (jnp.dot(h, fw2, preferred_element_type=f32, precision=HIGHEST) + fb2))
    h = jnp.sin(freq[2:3, :] * (jnp.dot(h, fw3, preferred_element_type=f32, precision=HIGHEST) + fb3))
    hh = h.astype(bf16)
    hl = (h - hh.astype(f32)).astype(bf16)
    h3 = jnp.concatenate([hh, hl, hh], axis=1)
    a = jnp.dot(h3[:nfirst], wo_first, preferred_element_type=f32)
    b = jnp.dot(h3[nfirst:], wo_second, preferred_element_type=f32)
    taps = jnp.concatenate([a, b], axis=0)
    return taps * jnp.exp(-z[:, 0:1] * deltas) * z[:, EMB_PAD - 1:EMB_PAD]


def _ctx_filter_kernel(z_ref, fw1, fb1, fw2, fb2, fw3, fb3, wo0, wo1, freq, dl, o_ref, *, Lc):
    taps = _filter_taps(z_ref[...], fw1[...], fb1[...], fw2[...], fb2[...], fw3[...], fb3[...],
                        wo1[...], wo0[...], freq[...], dl[...], Lc)
    o_ref[...] = taps / jnp.sum(jnp.abs(taps), axis=0, keepdims=True)


def _ctx_filter(ztab, fl, Lc, h):
    fw1, fb1, fw2, fb2, fw3, fb3, fwout, freq, deltas = fl
    cb = min(512, 2 * h)
    nb = (2 * h) // cb
    full = lambda a: pl.BlockSpec(a.shape, lambda j: (0,) * a.ndim)
    return pl.pallas_call(
        functools.partial(_ctx_filter_kernel, Lc=Lc),
        out_shape=jax.ShapeDtypeStruct((2 * Lc, 2 * h), f32),
        grid=(nb,),
        in_specs=[full(ztab), full(fw1), full(fb1), full(fw2), full(fb2), full(fw3), full(fb3),
                  pl.BlockSpec((fwout.shape[0], cb), lambda j: (0, j)),
                  pl.BlockSpec((fwout.shape[0], cb), lambda j: (0, nb + j)),
                  full(freq), pl.BlockSpec((1, cb), lambda j: (0, j))],
        out_specs=pl.BlockSpec((2 * Lc, cb), lambda j: (0, j)),
        compiler_params=_cp(("arbitrary",)),
        name="ctx_filter",
    )(ztab, fw1, fb1, fw2, fb2, fw3, fb3, fwout, fwout, freq, deltas)


def _ctx_conv_kernel(v_ref, x1_ref, x2_ref, k1_ref, k2_ref, b1_ref, b2_ref, o_ref, u_scr, *, Lc):
    def conv(kk_ref):
        def body(s, acc):
            return acc + kk_ref[pl.ds(Lc - s, Lc), :] * u_scr[pl.ds(s, 1), :]
        return lax.fori_loop(0, Lc, body, jnp.zeros((Lc, 128), f32))

    v = v_ref[...]
    u_scr[...] = v
    z = x1_ref[...] * (conv(k1_ref) + b1_ref[...] * v)
    u_scr[...] = z
    o_ref[...] = x2_ref[...] * (conv(k2_ref) + b2_ref[...] * z)


def _ctx_conv(u_ctx, kk, bias, B, Lc):
    h = u_ctx.shape[1] // 3
    nb = h // 128
    return pl.pallas_call(
        functools.partial(_ctx_conv_kernel, Lc=Lc),
        out_shape=jax.ShapeDtypeStruct((B * Lc, h), f32),
        grid=(B, nb),
        in_specs=[
            pl.BlockSpec((Lc, 128), lambda b, j: (b, j)),
            pl.BlockSpec((Lc, 128), lambda b, j: (b, nb + j)),
            pl.BlockSpec((Lc, 128), lambda b, j: (b, 2 * nb + j)),
            pl.BlockSpec((2 * Lc, 128), lambda b, j: (0, j)),
            pl.BlockSpec((2 * Lc, 128), lambda b, j: (0, nb + j)),
            pl.BlockSpec((1, 128), lambda b, j: (0, j)),
            pl.BlockSpec((1, 128), lambda b, j: (0, nb + j)),
        ],
        out_specs=pl.BlockSpec((Lc, 128), lambda b, j: (b, j)),
        scratch_shapes=[pltpu.VMEM((Lc, 128), f32)],
        compiler_params=_cp(("arbitrary", "arbitrary")),
        name="ctx_conv",
    )(u_ctx, u_ctx, u_ctx, kk, kk, bias.reshape(1, 2 * h), bias.reshape(1, 2 * h))


def _dft_constants(L):
    N2 = FFT_N2
    N1h = L // N2
    N1 = 2 * N1h
    N = N1 * N2
    k1 = jnp.arange(N1, dtype=i32)
    n1 = jnp.arange(N1, dtype=i32)
    n2 = jnp.arange(N2, dtype=i32)
    ph = (k1[None, :, None] * (n1[None, None, :] * N2 + n2[:, None, None])) % N
    ang = ph.astype(f32) * (2.0 * math.pi / N)
    gr, gi = jnp.cos(ang), -jnp.sin(ang)
    grh, gih = gr[:, :, :N1h], gi[:, :, :N1h]
    g_fwd = jnp.concatenate([jnp.concatenate([grh, -gih], axis=2),
                             jnp.concatenate([gih, grh], axis=2)], axis=1).astype(bf16)
    g_real = jnp.concatenate([gr, gi], axis=1).astype(bf16)
    mr = jnp.swapaxes(grh, 1, 2)
    mi = -jnp.swapaxes(gih, 1, 2)
    g_inv = jnp.concatenate([jnp.concatenate([mr, -mi], axis=2),
                             jnp.concatenate([mi, mr], axis=2)], axis=1).astype(bf16)
    kk = jnp.arange(N2, dtype=i32)
    a2 = ((kk[:, None] * kk[None, :]) % N2).astype(f32) * (2.0 * math.pi / N2)
    fr, fi = jnp.cos(a2), -jnp.sin(a2)
    fb_fwd = jnp.concatenate([jnp.concatenate([fr, -fi], axis=1),
                              jnp.concatenate([fi, fr], axis=1)], axis=0).astype(bf16)
    fb_inv = jnp.concatenate([jnp.concatenate([fr, fi], axis=1),
                              jnp.concatenate([-fi, fr], axis=1)], axis=0).astype(bf16)
    return dict(N1h=N1h, N1=N1, N2=N2, N=N, g_fwd=g_fwd, g_real=g_real, g_inv=g_inv, fb_fwd=fb_fwd, fb_inv=fb_inv)


def _strided_rows(refs, start, count, stride):
    parts = []
    for ref in refs:
        assert ref.shape[-1] == LANES
        flat = ref.reshape(math.prod(ref.shape[:-1]), LANES)
        parts.append(flat[pl.ds(start, count, stride=stride), :])
    return parts[0] if len(parts) == 1 else jnp.concatenate(parts, axis=1)


def _pick(refs, j):
    n = refs[0].shape[0]
    return jnp.concatenate([_strided_rows(refs, j, n, 16), _strided_rows(refs, 8 + j, n, 16)], axis=0)


def _filt_a_kernel(z_ref, fw1, fb1, fw2, fb2, fw3, fb3, wo0, wo1, freq, dl, g_ref, y_ref, l1_ref, *, N1h):
    taps = _filter_taps(z_ref[...], fw1[...], fb1[...], fw2[...], fb2[...], fw3[...], fb3[...],
                        wo0[...], wo1[...], freq[...], dl[...], N1h)

    @pl.when(pl.program_id(0) == 0)
    def _():
        l1_ref[...] = jnp.zeros_like(l1_ref)

    l1_ref[...] += jnp.sum(jnp.abs(taps), axis=0, keepdims=True)
    y_ref[...] = jnp.dot(g_ref[...], taps.astype(bf16), preferred_element_type=f32)


def _filt_b_kernel(ya_ref, yb_ref, fb_ref, l1_ref, k_ref, *, N):
    scale = 1.0 / (l1_ref[...] * float(N))
    for j in range(8):
        k_ref[j] = jnp.dot(fb_ref[...], _pick((ya_ref, yb_ref), j).astype(bf16), preferred_element_type=f32) * scale


def _latent_filter(ztab, fl, dc, h):
    fw1, fb1, fw2, fb2, fw3, fb3, fwout, freq, deltas = fl
    N1h, N1, N2, N = dc["N1h"], dc["N1"], dc["N2"], dc["N"]
    full = lambda a: pl.BlockSpec(a.shape, lambda n: (0,) * a.ndim)
    yk, l1 = pl.pallas_call(
        functools.partial(_filt_a_kernel, N1h=N1h),
        out_shape=(jax.ShapeDtypeStruct((N2, 2 * N1, 2 * h), f32), jax.ShapeDtypeStruct((1, 2 * h), f32)),
        grid=(N2,),
        in_specs=[pl.BlockSpec((None, N1, EMB_PAD), lambda n: (n, 0, 0)),
                  full(fw1), full(fb1), full(fw2), full(fb2), full(fw3), full(fb3),
                  pl.BlockSpec((fwout.shape[0], 2 * h), lambda n: (0, 0)),
                  pl.BlockSpec((fwout.shape[0], 2 * h), lambda n: (0, 1)),
                  full(freq), full(deltas),
                  pl.BlockSpec((None, 2 * N1, N1), lambda n: (n, 0, 0))],
        out_specs=[pl.BlockSpec((None, 2 * N1, 2 * h), lambda n: (n, 0, 0)),
                   pl.BlockSpec((1, 2 * h), lambda n: (0, 0))],
        compiler_params=_cp(("arbitrary",)),
        name="filter_stage_a",
    )(ztab, fw1, fb1, fw2, fb2, fw3, fb3, fwout, fwout, freq, deltas, dc["g_real"])
    cb = 2 * LANES
    yk4 = yk.reshape(N2, 2, N1, 2 * h)
    return pl.pallas_call(
        functools.partial(_filt_b_kernel, N=N),
        out_shape=jax.ShapeDtypeStruct((N1, 2 * N2, 2 * h), f32),
        grid=(N1 // 8, (2 * h) // cb),
        in_specs=[pl.BlockSpec((N2, 2, 8, LANES), lambda i, c: (0, 0, i, 2 * c)),
                  pl.BlockSpec((N2, 2, 8, LANES), lambda i, c: (0, 0, i, 2 * c + 1)),
                  pl.BlockSpec((2 * N2, 2 * N2), lambda i, c: (0, 0)),
                  pl.BlockSpec((1, cb), lambda i, c: (0, c))],
        out_specs=pl.BlockSpec((8, 2 * N2, cb), lambda i, c: (i, 0, c)),
        compiler_params=_cp(("arbitrary", "arbitrary")),
        name="filter_stage_b",
    )(yk4, yk4, dc["fb_fwd"], l1)


def _stack_batches(ref, j):
    return _strided_rows((ref,), j, ref.shape[0] * ref.shape[1], 8)


def _conv_a_kernel(u_ref, g_ref, y_ref):
    for j in range(8):
        y_ref[j] = jnp.dot(g_ref[j], _stack_batches(u_ref, j).astype(bf16), preferred_element_type=f32)


def _conv_b_kernel(ya_ref, yb_ref, fbf_ref, fbi_ref, k_ref, w_ref, *, N2):
    for j in range(8):
        z = jnp.dot(fbf_ref[...], _pick((ya_ref, yb_ref), j).astype(bf16), preferred_element_type=f32)
        zr, zi = z[:N2], z[N2:]
        kr, ki = k_ref[j, :N2, :], k_ref[j, N2:, :]
        p = jnp.concatenate([zr * kr - zi * ki, zr * ki + zi * kr], axis=0).astype(bf16)
        w_ref[j] = jnp.dot(fbi_ref[...], p, preferred_element_type=f32)


def _conv_mid_kernel(w_ref, gi_ref, g_ref, v_ref, x1_ref, b_ref, z_ref, y_ref):
    for j in range(8):
        y = jnp.dot(gi_ref[j], _pick((w_ref,), j).astype(bf16), preferred_element_type=f32)
        vv = _stack_batches(v_ref, j)
        z = _stack_batches(x1_ref, j) * (y + b_ref[...] * vv)
        z_ref[j] = z
        y_ref[j] = jnp.dot(g_ref[j], z.astype(bf16), preferred_element_type=f32)


def _store_strided_rows(ref, start, stride, val):
    assert ref.shape[-1] == LANES
    flat = ref.reshape(math.prod(ref.shape[:-1]), LANES)
    flat[pl.ds(start, val.shape[0], stride=stride), :] = val


def _conv_out_kernel(w_ref, gi_ref, z_ref, x2_ref, b_ref, o_ref):
    for j in range(8):
        y = jnp.dot(gi_ref[j], _pick((w_ref,), j).astype(bf16), preferred_element_type=f32)
        o = _stack_batches(x2_ref, j) * (y + b_ref[...] * z_ref[j])
        _store_strided_rows(o_ref, j, 8, o)


def _latent_hyena(u_lat, kf, bias, dc, B, L):
    assert B == 2, "the long convolution packs exactly two batch rows into one complex sequence"
    h = u_lat.shape[1] // 3
    N1h, N1, N2 = dc["N1h"], dc["N1"], dc["N2"]
    u4 = u_lat.reshape(B, N1h, N2, 3 * h)
    cb = LANES
    nb = h // cb
    cb2 = 2 * LANES
    nb2 = h // cb2
    bias2 = bias.reshape(1, 2 * h)
    ublk = lambda col: pl.BlockSpec((B, N1h, 8, cb), lambda i, c: (0, 0, i, col * nb + c))

    ya = pl.pallas_call(
        _conv_a_kernel,
        out_shape=jax.ShapeDtypeStruct((N2, 2 * N1, h), f32),
        grid=(N2 // 8, nb),
        in_specs=[ublk(0), pl.BlockSpec((8, 2 * N1, 2 * N1h), lambda i, c: (i, 0, 0))],
        out_specs=pl.BlockSpec((8, 2 * N1, cb), lambda i, c: (i, 0, c)),
        compiler_params=_cp(("arbitrary", "arbitrary")),
        name="conv_stage_a",
    )(u4, dc["g_fwd"])

    def stage_b(y, order):
        y4 = y.reshape(N2, 2, N1, h)
        return pl.pallas_call(
            functools.partial(_conv_b_kernel, N2=N2),
            out_shape=jax.ShapeDtypeStruct((N1, 2 * N2, h), f32),
            grid=(N1 // 8, nb2),
            in_specs=[pl.BlockSpec((N2, 2, 8, LANES), lambda i, c: (0, 0, i, 2 * c)),
                      pl.BlockSpec((N2, 2, 8, LANES), lambda i, c: (0, 0, i, 2 * c + 1)),
                      pl.BlockSpec((2 * N2, 2 * N2), lambda i, c: (0, 0)),
                      pl.BlockSpec((2 * N2, 2 * N2), lambda i, c: (0, 0)),
                      pl.BlockSpec((8, 2 * N2, cb2), lambda i, c: (i, 0, order * nb2 + c))],
            out_specs=pl.BlockSpec((8, 2 * N2, cb2), lambda i, c: (i, 0, c)),
            compiler_params=_cp(("arbitrary", "arbitrary")),
            name="conv_stage_b",
        )(y4, y4, dc["fb_fwd"], dc["fb_inv"], kf)

    w1 = stage_b(ya, 0)
    zp, ya2 = pl.pallas_call(
        _conv_mid_kernel,
        out_shape=(jax.ShapeDtypeStruct((N2, 2 * N1h, h), f32), jax.ShapeDtypeStruct((N2, 2 * N1, h), f32)),
        grid=(N2 // 8, nb),
        in_specs=[pl.BlockSpec((N1, 2, 8, cb), lambda i, c: (0, 0, i, c)),
                  pl.BlockSpec((8, 2 * N1h, 2 * N1), lambda i, c: (i, 0, 0)),
                  pl.BlockSpec((8, 2 * N1, 2 * N1h), lambda i, c: (i, 0, 0)),
                  ublk(0), ublk(1),
                  pl.BlockSpec((1, cb), lambda i, c: (0, c))],
        out_specs=[pl.BlockSpec((8, 2 * N1h, cb), lambda i, c: (i, 0, c)),
                   pl.BlockSpec((8, 2 * N1, cb), lambda i, c: (i, 0, c))],
        compiler_params=_cp(("arbitrary", "arbitrary")),
        name="conv_stage_mid",
    )(w1.reshape(N1, 2, N2, h), dc["g_inv"], dc["g_fwd"], u4, u4, bias2)
    w2 = stage_b(ya2, 1)
    y = pl.pallas_call(
        _conv_out_kernel,
        out_shape=jax.ShapeDtypeStruct((B, N1h, N2, h), f32),
        grid=(N2 // 8, nb),
        in_specs=[pl.BlockSpec((N1, 2, 8, cb), lambda i, c: (0, 0, i, c)),
                  pl.BlockSpec((8, 2 * N1h, 2 * N1), lambda i, c: (i, 0, 0)),
                  pl.BlockSpec((8, 2 * N1h, cb), lambda i, c: (i, 0, c)),
                  ublk(2),
                  pl.BlockSpec((1, cb), lambda i, c: (0, nb + c))],
        out_specs=pl.BlockSpec((B, N1h, 8, cb), lambda i, c: (0, 0, i, c)),
        compiler_params=_cp(("arbitrary", "arbitrary")),
        name="conv_stage_out",
    )(w2.reshape(N1, 2, N2, h), dc["g_inv"], zp, u4, bias2)
    return y.reshape(B * L, h)


def _merge_kernel(of_ref, ob_ref, og_ref, ng_ref, yl_ref, yc_ref, gt_hy_ref, gt_hg_ref, phy_ref, phg_ref, u_ref,
                  *, H, nlat):
    i = pl.program_id(0)
    o = of_ref[...].astype(f32) + ob_ref[...].astype(f32)
    og = og_ref[...].astype(f32)
    parts = []
    for h in range(H):
        hs = slice(h * HEAD_DIM, (h + 1) * HEAD_DIM)
        oh = o[:, hs]
        r = lax.rsqrt(jnp.mean(oh * oh, axis=-1, keepdims=True) + RMS_EPS)
        parts.append((oh * r * ng_ref[...] * og[:, hs]).astype(bf16))
    y_hg = jnp.concatenate(parts, axis=1)
    y_hy = jnp.where(i < nlat, yl_ref[...], yc_ref[...]).astype(bf16)
    a = jnp.dot(y_hy, phy_ref[...], preferred_element_type=f32)
    b = jnp.dot(y_hg, phg_ref[...], preferred_element_type=f32)
    u_ref[...] = (gt_hy_ref[...].astype(f32) * a + gt_hg_ref[...].astype(f32) * b).astype(u_ref.dtype)


def _merge(o_f, o_b, hg, ng, y_lat, y_ctx, gates, phy, phg, tm, ntile):
    h = o_f.shape[1]
    D = phy.shape[1]
    H = h // HEAD_DIM
    nlat = y_lat.shape[0] // tm
    nctx = y_ctx.shape[0] // tm
    return pl.pallas_call(
        functools.partial(_merge_kernel, H=H, nlat=nlat),
        out_shape=jax.ShapeDtypeStruct((ntile * tm, D), bf16),
        grid=(ntile,),
        in_specs=[
            pl.BlockSpec((tm, h), lambda i: (i, 0)),
            pl.BlockSpec((tm, h), lambda i: (i, 0)),
            pl.BlockSpec((tm, h), lambda i: (i, COL_OG)),
            pl.BlockSpec((1, HEAD_DIM), lambda i: (0, 0)),
            pl.BlockSpec((tm, h), lambda i: (jnp.minimum(i, nlat - 1), 0)),
            pl.BlockSpec((tm, h), lambda i: (jnp.clip(i - nlat, 0, nctx - 1), 0)),
            pl.BlockSpec((tm, D), lambda i: (i, COL_GATE // 2)),
            pl.BlockSpec((tm, D), lambda i: (i, COL_GATE // 2 + 1)),
            pl.BlockSpec((h, D), lambda i: (0, 0)),
            pl.BlockSpec((h, D), lambda i: (0, 0)),
        ],
        out_specs=pl.BlockSpec((tm, D), lambda i: (i, 0)),
        compiler_params=_cp(("arbitrary",)),
        name="merge",
    )(o_f, o_b, hg, ng, y_lat, y_ctx, gates, gates, phy, phg)


def _norm_mod(x, g, shift, scale):
    ms = jnp.mean(x * x, axis=-1, keepdims=True)
    return (x * lax.rsqrt(ms + RMS_EPS) * g) * (1.0 + scale) + shift


def _outproj_kernel(seg_ref, u_ref, w_ref, x_ref, mod_ref, g2_ref, rh_ref, rl_ref, xo_ref, lo_ref, *, D):
    seg = seg_ref[pl.program_id(0)]
    mix = jnp.dot(u_ref[...], w_ref[...], preferred_element_type=f32)
    gate = mod_ref[pl.ds(seg, 1), 2 * D:3 * D]
    xn = x_ref[...] + gate * mix
    xo_ref[...] = xn
    t2 = _norm_mod(xn, g2_ref[...], mod_ref[pl.ds(seg, 1), 3 * D:4 * D], mod_ref[pl.ds(seg, 1), 4 * D:5 * D])
    th = t2.astype(bf16)
    tl = (t2 - th.astype(f32)).astype(bf16)
    lo_ref[...] = (lax.dot_general(rh_ref[...], th, _NT, preferred_element_type=f32)
                   + lax.dot_general(rh_ref[...], tl, _NT, preferred_element_type=f32)
                   + lax.dot_general(rl_ref[...], th, _NT, preferred_element_type=f32))


def _outproj(u, w_out, xs, tile_seg, mod_l, g2, rh, rl, tm, ntile):
    D = w_out.shape[0]
    E = rh.shape[0]
    return pl.pallas_call(
        functools.partial(_outproj_kernel, D=D),
        out_shape=(jax.ShapeDtypeStruct((ntile * tm, D), f32), jax.ShapeDtypeStruct((E, ntile * tm), f32)),
        grid_spec=pltpu.PrefetchScalarGridSpec(
            num_scalar_prefetch=1,
            grid=(ntile,),
            in_specs=[
                pl.BlockSpec((tm, D), lambda i, s: (i, 0)),
                pl.BlockSpec((D, D), lambda i, s: (0, 0)),
                pl.BlockSpec((tm, D), lambda i, s: (i, 0)),
                pl.BlockSpec((8, 6 * D), lambda i, s: (0, 0)),
                pl.BlockSpec((1, D), lambda i, s: (0, 0)),
                pl.BlockSpec((E, D), lambda i, s: (0, 0)),
                pl.BlockSpec((E, D), lambda i, s: (0, 0)),
            ],
            out_specs=[pl.BlockSpec((tm, D), lambda i, s: (i, 0)),
                       pl.BlockSpec((E, tm), lambda i, s: (0, i))],
        ),
        compiler_params=_cp(("arbitrary",)),
        name="outproj",
    )(tile_seg, u, w_out, xs, mod_l, g2, rh, rl)


_PAIRS = ((0, 1), (0, 2), (0, 3), (1, 2), (1, 3), (2, 3))


def _route_kernel(lo_ref, rb_ref, o_ref, *, E):
    per = E // N_GROUPS
    assert per == 4
    lo = lo_ref[...]
    sc = jax.nn.sigmoid(lo)
    sel = sc + rb_ref[...]
    srow = [sel[e:e + 1, :] for e in range(E)]
    crow = [sc[e:e + 1, :] for e in range(E)]
    gs = []
    for g in range(N_GROUPS):
        x = srow[per * g:per * g + per]
        m = x[0] + x[1]
        for (a, b) in _PAIRS[1:]:
            m = jnp.maximum(m, x[a] + x[b])
        gs.append(m)
    gbest = jnp.zeros_like(gs[0]).astype(i32)
    best = gs[0]
    for g in range(1, N_GROUPS):
        better = gs[g] > best
        gbest = jnp.where(better, g, gbest)
        best = jnp.where(better, gs[g], best)

    def pick(rows, i):
        out = rows[i]
        for g in range(1, N_GROUPS):
            out = jnp.where(gbest == g, rows[per * g + i], out)
        return out

    x = [pick(srow, i) for i in range(per)]
    s = [pick(crow, i) for i in range(per)]
    chosen = []
    for i in range(per):
        cnt = jnp.zeros_like(gbest)
        for j in range(per):
            if j == i:
                continue
            beats = (x[j] >= x[i]) if j < i else (x[j] > x[i])
            cnt = cnt + jnp.where(beats, 1, 0)
        chosen.append(cnt < 2)
    pair = jnp.zeros_like(gbest)
    wa = jnp.zeros_like(best)
    wb = jnp.zeros_like(best)
    for p, (a, b) in enumerate(_PAIRS):
        hit = jnp.where(chosen[a], jnp.where(chosen[b], 1, 0), 0) == 1
        pair = jnp.where(hit, p, pair)
        wa = jnp.where(hit, s[a], wa)
        wb = jnp.where(hit, s[b], wb)
    tot = wa + wb
    cls = (gbest * len(_PAIRS) + pair).astype(f32)
    o_ref[...] = jnp.concatenate([cls, wa / tot, wb / tot, jnp.zeros((5, cls.shape[1]), f32)], axis=0)


def _route(logits_t, router_b, tl):
    E, T = logits_t.shape
    return pl.pallas_call(
        functools.partial(_route_kernel, E=E),
        out_shape=jax.ShapeDtypeStruct((8, T), f32),
        grid=(T // tl,),
        in_specs=[pl.BlockSpec((E, tl), lambda i: (0, i)), pl.BlockSpec((E, 1), lambda i: (0, 0))],
        out_specs=pl.BlockSpec((8, tl), lambda i: (0, i)),
        compiler_params=_cp(("arbitrary",)),
        name="route",
    )(logits_t, router_b.reshape(E, 1))


def _moe_kernel(be_ref, nb_ref, x_ref, rw_ref, mod_ref, g2_ref, w1_ref, w3_ref, w2_ref, o_ref, h_scr, acc_scr, *, D, nseg):
    i = pl.program_id(0)
    k = pl.program_id(1)
    f = pl.program_id(2)
    nf = pl.num_programs(2)
    live = i < nb_ref[0]

    def seg_rows(col0):
        seg = rw_ref[:, 2:3]
        out = mod_ref[0:1, col0:col0 + D]
        for r in range(1, nseg):
            out = jnp.where(seg == float(r), mod_ref[r:r + 1, col0:col0 + D], out)
        return out

    @pl.when(live & (k == 0) & (f == 0))
    def _():
        h_scr[...] = _norm_mod(x_ref[...], g2_ref[...], seg_rows(3 * D), seg_rows(4 * D)).astype(bf16)
        acc_scr[...] = jnp.zeros_like(acc_scr)

    @pl.when(live)
    def _():
        hx = h_scr[...]
        a = jnp.dot(hx, w1_ref[...], preferred_element_type=f32)
        b = jnp.dot(hx, w3_ref[...], preferred_element_type=f32)
        wsel = jnp.where(k == 0, rw_ref[:, 0:1], rw_ref[:, 1:2])
        act = (_silu(a) * b * wsel).astype(bf16)
        acc_scr[...] += jnp.dot(act, w2_ref[...], preferred_element_type=f32)

    @pl.when(live & (k == 1) & (f == nf - 1))
    def _():
        o_ref[...] = x_ref[...] + seg_rows(5 * D) * acc_scr[...]

    @pl.when(jnp.logical_not(live) & (k == 1) & (f == nf - 1))
    def _():
        o_ref[...] = x_ref[...]


def _moe(xg, rw, blk_e, nblk_used, mod_l, g2, w1, w3, w2, nseg):
    P, D = xg.shape
    F = w1.shape[2]
    tf = min(512, F)
    nblk = P // MOE_ROWS
    nf = F // tf

    def wsel(i, k, f, be, nb):
        live = i < nb[0]
        e = be[2 * jnp.minimum(i, nb[0] - 1) + jnp.where(live, k, 1)]
        return e, jnp.where(live, f, nf - 1)

    def w13(i, k, f, be, nb):
        e, ff = wsel(i, k, f, be, nb)
        return (e, 0, ff)

    def w2m(i, k, f, be, nb):
        e, ff = wsel(i, k, f, be, nb)
        return (e, ff, 0)

    return pl.pallas_call(
        functools.partial(_moe_kernel, D=D, nseg=nseg),
        out_shape=jax.ShapeDtypeStruct((P, D), f32),
        grid_spec=pltpu.PrefetchScalarGridSpec(
            num_scalar_prefetch=2,
            grid=(nblk, 2, nf),
            in_specs=[
                pl.BlockSpec((MOE_ROWS, D), lambda i, k, f, be, nb: (i, 0)),
                pl.BlockSpec((MOE_ROWS, 128), lambda i, k, f, be, nb: (i, 0)),
                pl.BlockSpec((8, 6 * D), lambda i, k, f, be, nb: (0, 0)),
                pl.BlockSpec((1, D), lambda i, k, f, be, nb: (0, 0)),
                pl.BlockSpec((None, D, tf), w13),
                pl.BlockSpec((None, D, tf), w13),
                pl.BlockSpec((None, tf, D), w2m),
            ],
            out_specs=pl.BlockSpec((MOE_ROWS, D), lambda i, k, f, be, nb: (i, 0)),
            scratch_shapes=[pltpu.VMEM((MOE_ROWS, D), bf16), pltpu.VMEM((MOE_ROWS, D), f32)],
        ),
        compiler_params=_cp(("arbitrary", "arbitrary", "arbitrary")),
        name="moe_experts",
    )(blk_e, nblk_used, xg, rw, mod_l, g2, w1, w3, w2)


def _final_norm_kernel(x_ref, g_ref, o_ref):
    x = x_ref[...]
    ms = jnp.mean(x * x, axis=-1, keepdims=True)
    o_ref[...] = x * lax.rsqrt(ms + RMS_EPS) * g_ref[...]


def _final_norm(x, g, tm):
    n, D = x.shape
    return pl.pallas_call(
        _final_norm_kernel,
        out_shape=jax.ShapeDtypeStruct((n, D), f32),
        grid=(n // tm,),
        in_specs=[pl.BlockSpec((tm, D), lambda i: (i, 0)), pl.BlockSpec((1, D), lambda i: (0, 0))],
        out_specs=pl.BlockSpec((tm, D), lambda i: (i, 0)),
        compiler_params=_cp(("arbitrary",)),
        name="final_norm",
    )(x, g)


def _feature_rows(pos, valid, Lx, bands):
    t = jnp.linspace(0.0, 1.0, Lx, dtype=f32)[jnp.clip(pos, 0, Lx - 1)][..., None]
    w = ((2.0 * math.pi / Lx) * jnp.clip(pos, 0, Lx - 1).astype(f32))[..., None]
    fb = jnp.linspace(1e-4, bands - 1, bands, dtype=f32)
    z = jnp.concatenate([t, jnp.cos(fb * w), -jnp.sin(fb * w)], axis=-1)
    z = jnp.pad(z, [(0, 0)] * (z.ndim - 1) + [(0, EMB_PAD - 1 - z.shape[-1])])
    return jnp.concatenate([z, valid.astype(f32)[..., None]], axis=-1)


def _latent_ztab(L, bands, dc):
    N1, N2 = dc["N1"], dc["N2"]
    tau = jnp.arange(N1, dtype=i32)[None, :] * N2 + jnp.arange(N2, dtype=i32)[:, None]
    pos = jnp.where(tau < L, tau, 2 * L - tau)
    return _feature_rows(pos, tau != L, L, bands)


def _ctx_ztab(Lc, bands):
    r = jnp.arange(2 * Lc, dtype=i32)
    return _feature_rows(jnp.abs(r - Lc), r != 0, Lc, bands)


def kernel(x, c, ctx, c_ctx, ada_w, ada_b, norm1_g, norm2_g, final_g, w_in, hy_conv_w, hy_conv_b, hy_fw1, hy_fb1,
           hy_fw2, hy_fb2, hy_fw3, hy_fb3, hy_fwout, hy_freq, hy_bias, hg_lb_raw, hg_norm_g, p_hy, p_hg, w_out,
           router_w, router_b, moe_w1, moe_w3, moe_w2):
    B, L, D = x.shape
    Lc = ctx.shape[1]
    depth = ada_w.shape[0]
    h = D // 2
    E = router_w.shape[1]
    emb = hy_fw1.shape[1]
    bands = (emb - 1) // 2
    BL, BLc = B * L, B * Lc
    T = BL + BLc
    tm = BLc
    assert L % tm == 0 and L % FFT_N2 == 0 and Lc % SCAN_CHUNK == 0 and h % HEAD_DIM == 0 and B + 1 <= 8
    nlat = BL // tm
    ntile = T // tm
    tile_seg = jnp.concatenate([jnp.repeat(jnp.arange(B, dtype=i32), L // tm), jnp.full((1,), B, i32)])
    tmi = 2 * tm if L % (2 * tm) == 0 else tm
    T_pad = (T + tmi - 1) // tmi * tmi
    seg_in = jnp.minimum((jnp.arange(T_pad // tmi, dtype=i32) * tmi) // L, B)

    xs = jnp.concatenate([x.reshape(BL, D).astype(f32), ctx.reshape(BLc, D).astype(f32),
                          jnp.zeros((T_pad - T, D), f32)], axis=0)
    cond = jnp.concatenate([c.astype(f32), c_ctx.astype(f32)[None, :]], axis=0)
    mod = _modulation(cond, ada_w.astype(f32), ada_b.astype(f32))

    lb = jnp.cumsum(jax.nn.softmax(hg_lb_raw.astype(f32), axis=0), axis=0)
    lb = lb - lb[:1]

    dc = _dft_constants(L)
    zt_lat = _latent_ztab(L, bands, dc)
    zt_ctx = _ctx_ztab(Lc, bands)
    deltas = jnp.abs(jnp.linspace(HY_MIN_DECAY, HY_MAX_DECAY, 2 * h, dtype=f32)).reshape(1, 2 * h)

    rw_t = router_w.astype(f32).T
    rh = rw_t.astype(bf16)
    rl = (rw_t - rh.astype(f32)).astype(bf16)
    n_cls = N_GROUPS * len(_PAIRS)
    per = E // N_GROUPS
    cls_e = jnp.array([[per * g + a, per * g + b] for g in range(N_GROUPS) for (a, b) in _PAIRS], i32)

    for l in range(depth):
        last = l == depth - 1
        mod_l = mod[l]
        proj, lg = _inproj(xs, seg_in, mod_l, norm1_g[l].astype(f32).reshape(1, D), w_in[l].astype(bf16), lb[l], tmi)
        p_hyena = hg = gates = proj
        o_f, o_b = _hgrn_scan(hg, lg, B, L, Lc)

        pad = EMB_PAD - emb
        wo = hy_fwout[l].astype(f32)
        wo_hi = wo.astype(bf16)
        wo_lo = (wo - wo_hi.astype(f32)).astype(bf16)
        fl = (jnp.pad(hy_fw1[l].astype(f32), ((0, pad), (0, 0))), hy_fb1[l].astype(f32).reshape(1, -1),
              hy_fw2[l].astype(f32), hy_fb2[l].astype(f32).reshape(1, -1),
              hy_fw3[l].astype(f32), hy_fb3[l].astype(f32).reshape(1, -1),
              jnp.concatenate([wo_hi, wo_hi, wo_lo], axis=0), hy_freq[l].astype(f32), deltas)
        u_lat, u_ctx = _shortconv(p_hyena, hy_conv_w[l].astype(f32), hy_conv_b[l].astype(f32), B, L, Lc)
        kf = _latent_filter(zt_lat, fl, dc, h)
        y_lat = _latent_hyena(u_lat, kf, hy_bias[l].astype(f32), dc, B, L)
        if last:
            y_ctx = jnp.zeros((BLc, h), f32)
            nt = nlat
        else:
            kk = _ctx_filter(zt_ctx, fl, Lc, h)
            y_ctx = _ctx_conv(u_ctx, kk, hy_bias[l].astype(f32), B, Lc)
            nt = ntile
        u = _merge(o_f, o_b, hg, hg_norm_g[l].astype(f32).reshape(1, HEAD_DIM), y_lat, y_ctx, gates,
                   p_hy[l].astype(bf16), p_hg[l].astype(bf16), tm, nt)
        g2 = norm2_g[l].astype(f32).reshape(1, D)
        x1, logits_t = _outproj(u, w_out[l].astype(bf16), xs, tile_seg, mod_l, g2, rh, rl, tm, nt)

        Tm = nt * tm
        route = _route(logits_t, router_b.astype(f32), tm)
        cls = route[0].astype(i32)
        counts = jnp.sum((cls[:, None] == jnp.arange(n_cls, dtype=i32)[None, :]).astype(i32), axis=0)
        padded = (counts + MOE_ROWS - 1) // MOE_ROWS * MOE_ROWS
        pend = jnp.cumsum(padded)
        pstart = pend - padded
        sstart = jnp.cumsum(counts) - counts
        order = jnp.argsort(cls, stable=True).astype(i32)
        rank_of_tok = jnp.argsort(order).astype(i32)
        pos_of_tok = pstart[cls] + rank_of_tok - sstart[cls]
        P = (Tm + n_cls * (MOE_ROWS - 1) + MOE_ROWS - 1) // MOE_ROWS * MOE_ROWS
        nblk = P // MOE_ROWS
        blk_start = jnp.arange(nblk, dtype=i32) * MOE_ROWS
        blk_cls = jnp.minimum(jnp.sum((blk_start[:, None] >= pend[None, :]).astype(i32), axis=1), n_cls - 1)
        blk_e = cls_e[blk_cls].reshape(-1).astype(i32)
        nblk_used = (pend[-1] // MOE_ROWS).astype(i32).reshape(1)
        row_cls = jnp.repeat(blk_cls, MOE_ROWS)
        within = jnp.arange(P, dtype=i32) - pstart[row_cls]
        okf = (within < counts[row_cls]).astype(f32)
        src = order[jnp.clip(sstart[row_cls] + within, 0, Tm - 1)]
        seg_tok = jnp.repeat(tile_seg[:nt], tm).astype(f32)
        rw = jnp.stack([route[1][src] * okf, route[2][src] * okf, seg_tok[src]], axis=1)
        rw = jnp.pad(rw, ((0, 0), (0, 128 - 3)))
        xg = x1[src]
        yg = _moe(xg, rw, blk_e, nblk_used, mod_l, g2, moe_w1[l].astype(bf16), moe_w3[l].astype(bf16),
                  moe_w2[l].astype(bf16), B + 1)
        if last:
            xs = yg[pos_of_tok]
        else:
            xs = yg[jnp.concatenate([pos_of_tok, jnp.zeros((T_pad - T,), i32)])]

    out = _final_norm(xs[:BL], final_g.astype(f32).reshape(1, D), tm)
    return out.reshape(B, L, D).astype(x.dtype)
```

```python
import functools
import math

import jax
import jax.numpy as jnp
from jax import lax
from jax.experimental import pallas as pl
from jax.experimental.pallas import tpu as pltpu

f32 = jnp.float32
bf16 = jnp.bfloat16
i32 = jnp.int32

RMS_EPS = 1e-6
MXU_COLS = 256
LANES = 128
HEAD_DIM = 128
SCAN_CHUNK = 64
LOG2_E = 1.4426950408889634
DIAG_EXP2_CAP = 120.0
N_GROUPS = 4
HY_MAX_DECAY = math.log(1e-2) / 0.3
HY_MIN_DECAY = math.log(1e-2) / 1.5
FFT_N2 = 256
EMB_PAD = 64
FILT_Q = 2
MOE_ROWS = 512
VMEM_LIMIT = 52 * 1024 * 1024
CAST_BLOCK_BYTES = 8 * 1024 * 1024
HIGHEST = lax.Precision.HIGHEST

_NT = (((1,), (1,)), ((), ()))
_TN = (((0,), (0,)), ((), ()))


def _cp(sem, vmem=VMEM_LIMIT):
    return pltpu.CompilerParams(dimension_semantics=sem, vmem_limit_bytes=vmem)


def _silu(x):
    return x * jax.nn.sigmoid(x)


def _sigmoid_t(x):
    return 0.5 * jnp.tanh(0.5 * x) + 0.5


def _cast_kernel(x_ref, o_ref):
    o_ref[...] = x_ref[...].astype(o_ref.dtype)


def _to_bf16(w):
    shape = w.shape
    cols = shape[-1]
    rows = math.prod(shape[:-1])
    tr = 16
    while tr * 2 * cols * 4 <= CAST_BLOCK_BYTES and rows % (tr * 2) == 0:
        tr *= 2
    assert rows % tr == 0
    out = pl.pallas_call(
        _cast_kernel,
        out_shape=jax.ShapeDtypeStruct((rows, cols), bf16),
        grid=(rows // tr,),
        in_specs=[pl.BlockSpec((tr, cols), lambda i: (i, 0))],
        out_specs=pl.BlockSpec((tr, cols), lambda i: (i, 0)),
        compiler_params=_cp(("arbitrary",)),
        name="cast_bf16",
    )(w.reshape(rows, cols))
    return out.reshape(shape)


def _mod_kernel(sb_ref, w_ref, b_ref, o_ref, *, nrow, D, tn):
    rep = tn // 128

    def body(i, accs):
        k0 = pl.multiple_of(i * 8, 8)
        w = w_ref[pl.ds(k0, 8), :]
        out = []
        for m in range(nrow):
            sb = sb_ref[m, pl.ds(k0, 8), :]
            out.append(accs[m] + w * jnp.concatenate([sb] * rep, axis=1))
        return tuple(out)

    accs = lax.fori_loop(0, D // 8, body, tuple(jnp.zeros((8, tn), f32) for _ in range(nrow)), unroll=4)
    o_ref[...] = jnp.zeros_like(o_ref)
    for m in range(nrow):
        o_ref[m:m + 1, :] = jnp.sum(accs[m], axis=0, keepdims=True) + b_ref[...]


def _modulation(cond, ada_w, ada_b):
    nrow, D = cond.shape
    depth, _, n6 = ada_w.shape
    tn = 1024 if n6 % 1024 == 0 else n6
    sb = jnp.broadcast_to(_silu(cond)[:, :, None], (nrow, D, 128))
    return pl.pallas_call(
        functools.partial(_mod_kernel, nrow=nrow, D=D, tn=tn),
        out_shape=jax.ShapeDtypeStruct((depth, 8, n6), f32),
        grid=(depth, n6 // tn),
        in_specs=[
            pl.BlockSpec((nrow, D, 128), lambda l, j: (0, 0, 0)),
            pl.BlockSpec((None, D, tn), lambda l, j: (l, 0, j)),
            pl.BlockSpec((None, 1, tn), lambda l, j: (l, 0, j)),
        ],
        out_specs=pl.BlockSpec((None, 8, tn), lambda l, j: (l, 0, j)),
        compiler_params=_cp(("arbitrary", "arbitrary")),
        name="modulation",
    )(sb, ada_w, ada_b.reshape(depth, 1, n6))


def _inproj_kernel(seg_ref, x_ref, mod_ref, g_ref, w_ref, lb_ref, p_ref, lg_ref, hx_scr, *, D):
    i = pl.program_id(0)
    j = pl.program_id(1)

    @pl.when(j == 0)
    def _():
        seg = seg_ref[i]
        xf = x_ref[...]
        ms = jnp.mean(xf * xf, axis=-1, keepdims=True)
        y = xf * lax.rsqrt(ms + RMS_EPS) * g_ref[...]
        shift = mod_ref[pl.ds(seg, 1), 0:D]
        scale = mod_ref[pl.ds(seg, 1), D:2 * D]
        hx_scr[...] = (y * (1.0 + scale) + shift).astype(bf16)

    tn = w_ref.shape[1]
    pw = min(MXU_COLS, tn)

    def pieces(epilogue):
        for c in range(tn // pw):
            cs = slice(c * pw, (c + 1) * pw)
            epilogue(cs, jnp.dot(hx_scr[...], w_ref[:, cs], preferred_element_type=f32))

    @pl.when((j < 3) | (j == 4))
    def _():
        def epi(cs, acc):
            p_ref[:, cs] = acc.astype(p_ref.dtype)
        pieces(epi)

    @pl.when((j == 3) | (j == 7))
    def _():
        def epi(cs, acc):
            p_ref[:, cs] = (acc * _sigmoid_t(acc)).astype(p_ref.dtype)
        pieces(epi)

    @pl.when((j == 5) | (j == 6))
    def _():
        def epi(cs, acc):
            lb = lb_ref[pl.ds(j - 5, 1), cs]
            f = lb + (1.0 - lb) * jax.nn.sigmoid(acc)
            p_ref[:, cs] = (1.0 - f).astype(p_ref.dtype)
            lg_ref[:, cs] = jnp.log(f)
        pieces(epi)

    @pl.when(j >= 8)
    def _():
        def epi(cs, acc):
            p_ref[:, cs] = _sigmoid_t(acc).astype(p_ref.dtype)
        pieces(epi)


COL_HY, COL_Q, COL_V, COL_KF, COL_KB, COL_OG, COL_GATE = 0, 3, 4, 5, 6, 7, 8


def _inproj(xs, tile_seg, mod_l, g1, w_bf, layer, lb_l, tm):
    T, D = xs.shape
    h = D // 2
    return pl.pallas_call(
        functools.partial(_inproj_kernel, D=D),
        out_shape=(
            jax.ShapeDtypeStruct((T, 12 * h), bf16),
            jax.ShapeDtypeStruct((T, 2 * h), f32),
        ),
        grid_spec=pltpu.PrefetchScalarGridSpec(
            num_scalar_prefetch=1,
            grid=(T // tm, 12),
            in_specs=[
                pl.BlockSpec((tm, D), lambda i, j, s: (i, 0)),
                pl.BlockSpec((8, 6 * D), lambda i, j, s: (0, 0)),
                pl.BlockSpec((1, D), lambda i, j, s: (0, 0)),
                pl.BlockSpec((None, D, h), lambda i, j, s: (layer, 0, j)),
                pl.BlockSpec((2, h), lambda i, j, s: (0, 0)),
            ],
            out_specs=[
                pl.BlockSpec((tm, h), lambda i, j, s: (i, j)),
                pl.BlockSpec((tm, h), lambda i, j, s: (i, jnp.clip(j - COL_KF, 0, 1))),
            ],
            scratch_shapes=[pltpu.VMEM((tm, D), bf16)],
        ),
        compiler_params=_cp(("arbitrary", "arbitrary")),
        name="inproj",
    )(tile_seg, xs, mod_l, g1, w_bf, lb_l)


def _split3(g):
    g1 = g.astype(bf16)
    r1 = g - g1.astype(f32)
    g2 = r1.astype(bf16)
    g3 = (r1 - g2.astype(f32)).astype(bf16)
    return g1, g2, g3


def _chunk_cumsum(g_ref, rev):
    R = g_ref.shape[0]
    ti = lax.broadcasted_iota(i32, (R, R), 0)
    si = lax.broadcasted_iota(i32, (R, R), 1)
    inside = (ti // SCAN_CHUNK) == (si // SCAN_CHUNK)
    tri = jnp.where(inside, jnp.where((si >= ti) if rev else (si <= ti), 1.0, 0.0), 0.0).astype(bf16)
    return sum(jnp.dot(tri, gi, preferred_element_type=f32) for gi in _split3(g_ref[...] * LOG2_E))


def _hgrn_chunk(q_ref, v_ref, k_ref, b_all, o_ref, s_scr, r0, d, *, rev, H):
    C = SCAN_CHUNK
    rows = slice(r0, r0 + C)
    b = b_all[r0:r0 + C, :]
    W = b.shape[1]
    ti = lax.broadcasted_iota(i32, (C, C), 0)
    si = lax.broadcasted_iota(i32, (C, C), 1)
    causal = (si >= ti) if rev else (si <= ti)

    def rowb(r, n):
        return jnp.broadcast_to(b[r:r + 1, :], (n, W))

    off = 1 if rev else 0
    refs = [
        rowb(31 + off, 64),
        jnp.concatenate([rowb(15 + off, 32), rowb(47 + off, 32)], axis=0),
        jnp.concatenate([rowb(16 * i + 7 + off, 16) for i in range(4)], axis=0),
    ]
    qb = q_ref[rows, :]
    kb = k_ref[rows, :]
    v = v_ref[rows, :]
    qs, ks = [], []
    for m in refs:
        qs.append(qb * jnp.exp2(b - m).astype(bf16))
        ks.append(kb * jnp.exp2(m - b).astype(bf16))
    md = jnp.concatenate([rowb(8 * i + (7 if rev else 0), 8) for i in range(8)], axis=0)
    qs.append(qb * jnp.exp2(b - md).astype(bf16))
    ks.append(kb * jnp.exp2(jnp.minimum(md - b, DIAG_EXP2_CAP)).astype(bf16))

    bend = b[0:1, :] if rev else b[C - 1:C, :]
    q_in = qb * jnp.exp2(b).astype(bf16)
    k_st = kb * jnp.exp2(bend - b).astype(bf16)
    dec = jnp.exp2(bend)
    lvl = jnp.where((ti // 32) != (si // 32), 0,
                    jnp.where((ti // 16) != (si // 16), 1,
                              jnp.where((ti // 8) != (si // 8), 2, 3)))
    for h in range(H):
        hs = slice(h * HEAD_DIM, (h + 1) * HEAD_DIM)
        p = [lax.dot_general(qs[l][:, hs], ks[l][:, hs], _NT, preferred_element_type=f32) for l in range(4)]
        att = jnp.where(lvl == 0, p[0], jnp.where(lvl == 1, p[1], jnp.where(lvl == 2, p[2], p[3])))
        att = jnp.where(causal, att, 0.0).astype(bf16)
        st = s_scr[d, h]
        o_h = jnp.dot(att, v[:, hs], preferred_element_type=f32)
        o_h = o_h + lax.dot_general(q_in[:, hs], st.astype(bf16), _NT, preferred_element_type=f32)
        o_ref[rows, hs] = o_h.astype(o_ref.dtype)
        s_scr[d, h] = st * dec[:, hs] + lax.dot_general(v[:, hs], k_st[:, hs], _TN, preferred_element_type=f32)


def _hgrn_kernel(qf_ref, vf_ref, kf_ref, gf_ref, qb_ref, vb_ref, kb_ref, gb_ref, of_ref, ob_ref, s_scr, *, H, G):
    @pl.when(pl.program_id(1) == 0)
    def _():
        s_scr[...] = jnp.zeros_like(s_scr)

    bf = _chunk_cumsum(gf_ref, False)
    bb = _chunk_cumsum(gb_ref, True)
    for c in range(G):
        _hgrn_chunk(qf_ref, vf_ref, kf_ref, bf, of_ref, s_scr, c * SCAN_CHUNK, 0, rev=False, H=H)
        _hgrn_chunk(qb_ref, vb_ref, kb_ref, bb, ob_ref, s_scr, (G - 1 - c) * SCAN_CHUNK, 1, rev=True, H=H)


def _hgrn_scan(hg, lg, B, L, Lc):
    T = hg.shape[0]
    h = lg.shape[1] // 2
    H = h // HEAD_DIM
    G = min(4, Lc // SCAN_CHUNK)
    R = G * SCAN_CHUNK
    assert L % R == 0 and Lc % R == 0
    nL, nC = L // R, Lc // R

    def fwd(b, s):
        return jnp.where(s < nC, B * nL + b * nC + s, b * nL + (s - nC))

    def bwd(b, s):
        return jnp.where(s < nC, B * nL + b * nC + (nC - 1 - s), b * nL + (nL - 1 - (s - nC)))

    def spec(rowfn, col):
        return pl.BlockSpec((R, h), lambda b, s: (rowfn(b, s), col))

    return pl.pallas_call(
        functools.partial(_hgrn_kernel, H=H, G=G),
        out_shape=(jax.ShapeDtypeStruct((T, h), bf16), jax.ShapeDtypeStruct((T, h), bf16)),
        grid=(B, nC + nL),
        in_specs=[spec(fwd, COL_Q), spec(fwd, COL_V), spec(fwd, COL_KF), spec(fwd, 0),
                  spec(bwd, COL_Q), spec(bwd, COL_V), spec(bwd, COL_KB), spec(bwd, 1)],
        out_specs=[spec(fwd, 0), spec(bwd, 0)],
        scratch_shapes=[pltpu.VMEM((2, H, HEAD_DIM, HEAD_DIM), f32)],
        compiler_params=_cp(("arbitrary", "arbitrary")),
        name="hgrn_scan",
    )(hg, hg, hg, lg, hg, hg, hg, lg)


def _shortconv_kernel(p_ref, pv_ref, nx_ref, w_ref, b_ref, ul_ref, uc_ref, *, R, BL, L, Lc):
    i = pl.program_id(1)
    r0 = i * R
    lat = r0 < BL
    first = jnp.where(lat, (r0 % L) == 0, ((r0 - BL) % Lc) == 0)
    last = jnp.where(lat, ((r0 + R) % L) == 0, ((r0 + R - BL) % Lc) == 0)
    ri = lax.broadcasted_iota(i32, (R, LANES), 0)

    def conv_cols(o_ref):
        for c in range(p_ref.shape[1] // LANES):
            cs = slice(c * LANES, (c + 1) * LANES)
            p = p_ref[:, cs].astype(f32)
            prev_row = jnp.where(first, 0.0, pv_ref[15:16, cs].astype(f32))
            next_row = jnp.where(last, 0.0, nx_ref[0:1, cs].astype(f32))
            pm = jnp.where(ri == 0, prev_row, pltpu.roll(p, 1, 0))
            pp = jnp.where(ri == R - 1, next_row, pltpu.roll(p, R - 1, 0))
            o_ref[:, cs] = pm * w_ref[0:1, cs] + p * w_ref[1:2, cs] + pp * w_ref[2:3, cs] + b_ref[:, cs]

    @pl.when(lat)
    def _():
        conv_cols(ul_ref)

    @pl.when(jnp.logical_not(lat))
    def _():
        conv_cols(uc_ref)


def _shortconv(p_hy, conv_w, conv_b, B, L, Lc):
    h = p_hy.shape[1] // 12
    W3 = 3 * h
    BL, BLc = B * L, B * Lc
    T = BL + BLc
    R = min(256, Lc)
    nlat = BL // R
    return pl.pallas_call(
        functools.partial(_shortconv_kernel, R=R, BL=BL, L=L, Lc=Lc),
        out_shape=(jax.ShapeDtypeStruct((BL, W3), f32), jax.ShapeDtypeStruct((BLc, W3), f32)),
        grid=(3, T // R),
        in_specs=[
            pl.BlockSpec((R, h), lambda j, i: (i, j)),
            pl.BlockSpec((16, h), lambda j, i: (jnp.maximum(i * (R // 16) - 1, 0), j)),
            pl.BlockSpec((16, h), lambda j, i: (jnp.minimum((i + 1) * (R // 16), T // 16 - 1), j)),
            pl.BlockSpec((3, h), lambda j, i: (0, j)),
            pl.BlockSpec((1, h), lambda j, i: (0, j)),
        ],
        out_specs=[
            pl.BlockSpec((R, h), lambda j, i: (jnp.minimum(i, nlat - 1), j)),
            pl.BlockSpec((R, h), lambda j, i: (jnp.maximum(i - nlat, 0), j)),
        ],
        compiler_params=_cp(("arbitrary", "arbitrary")),
        name="shortconv",
    )(p_hy, p_hy, p_hy, conv_w, conv_b.reshape(1, W3))


def _filter_taps(z, fw1, fb1, fw2, fb2, fw3, fb3, wo_first, wo_second, freq, deltas, nfirst):
    h = jnp.sin(freq[0:1, :] * (jnp.dot(z, fw1, preferred_element_type=f32, precision=HIGHEST) + fb1))
    h = jnp.sin(freq[1:2, :] * (jnp.dot(h, fw2, preferred_element_type=f32, precision=HIGHEST) + fb2))
    h = jnp.sin(freq[2:3, :] * (jnp.dot(h, fw3, preferred_element_type=f32, precision=HIGHEST) + fb3))
    hh = h.astype(bf16)
    hl = (h - hh.astype(f32)).astype(bf16)
    h3 = jnp.concatenate([hh, hl, hh], axis=1)
    a = jnp.dot(h3[:nfirst], wo_first, preferred_element_type=f32)
    b = jnp.dot(h3[nfirst:], wo_second, preferred_element_type=f32)
    taps = jnp.concatenate([a, b], axis=0)
    return taps * jnp.exp(-z[:, 0:1] * deltas) * z[:, EMB_PAD - 1:EMB_PAD]


def _ctx_filter_kernel(z_ref, fw1, fb1, fw2, fb2, fw3, fb3, wo0, wo1, freq, dl, o_ref, *, Lc):
    taps = _filter_taps(z_ref[...], fw1[...], fb1[...], fw2[...], fb2[...], fw3[...], fb3[...],
                        wo1[...], wo0[...], freq[...], dl[...], Lc)
    o_ref[...] = taps / jnp.sum(jnp.abs(taps), axis=0, keepdims=True)


def _ctx_filter(ztab, fl, Lc, h):
    fw1, fb1, fw2, fb2, fw3, fb3, fwout, freq, deltas = fl
    cb = min(512, 2 * h)
    nb = (2 * h) // cb
    full = lambda a: pl.BlockSpec(a.shape, lambda j: (0,) * a.ndim)
    return pl.pallas_call(
        functools.partial(_ctx_filter_kernel, Lc=Lc),
        out_shape=jax.ShapeDtypeStruct((2 * Lc, 2 * h), f32),
        grid=(nb,),
        in_specs=[full(ztab), full(fw1), full(fb1), full(fw2), full(fb2), full(fw3), full(fb3),
                  pl.BlockSpec((fwout.shape[0], cb), lambda j: (0, j)),
                  pl.BlockSpec((fwout.shape[0], cb), lambda j: (0, nb + j)),
                  full(freq), pl.BlockSpec((1, cb), lambda j: (0, j))],
        out_specs=pl.BlockSpec((2 * Lc, cb), lambda j: (0, j)),
        compiler_params=_cp(("arbitrary",)),
        name="ctx_filter",
    )(ztab, fw1, fb1, fw2, fb2, fw3, fb3, fwout, fwout, freq, deltas)


def _ctx_conv_kernel(v_ref, x1_ref, x2_ref, k1_ref, k2_ref, b1_ref, b2_ref, o_ref, u_scr, *, Lc):
    def conv(kk_ref):
        def body(s, acc):
            return acc + kk_ref[pl.ds(Lc - s, Lc), :] * u_scr[pl.ds(s, 1), :]
        return lax.fori_loop(0, Lc, body, jnp.zeros((Lc, 128), f32))

    v = v_ref[...]
    u_scr[...] = v
    z = x1_ref[...] * (conv(k1_ref) + b1_ref[...] * v)
    u_scr[...] = z
    o_ref[...] = x2_ref[...] * (conv(k2_ref) + b2_ref[...] * z)


def _ctx_conv(u_ctx, kk, bias, B, Lc):
    h = u_ctx.shape[1] // 3
    nb = h // 128
    return pl.pallas_call(
        functools.partial(_ctx_conv_kernel, Lc=Lc),
        out_shape=jax.ShapeDtypeStruct((B * Lc, h), f32),
        grid=(B, nb),
        in_specs=[
            pl.BlockSpec((Lc, 128), lambda b, j: (b, j)),
            pl.BlockSpec((Lc, 128), lambda b, j: (b, nb + j)),
            pl.BlockSpec((Lc, 128), lambda b, j: (b, 2 * nb + j)),
            pl.BlockSpec((2 * Lc, 128), lambda b, j: (0, j)),
            pl.BlockSpec((2 * Lc, 128), lambda b, j: (0, nb + j)),
            pl.BlockSpec((1, 128), lambda b, j: (0, j)),
            pl.BlockSpec((1, 128), lambda b, j: (0, nb + j)),
        ],
        out_specs=pl.BlockSpec((Lc, 128), lambda b, j: (b, j)),
        scratch_shapes=[pltpu.VMEM((Lc, 128), f32)],
        compiler_params=_cp(("arbitrary", "arbitrary")),
        name="ctx_conv",
    )(u_ctx, u_ctx, u_ctx, kk, kk, bias.reshape(1, 2 * h), bias.reshape(1, 2 * h))


def _dft_constants(L):
    N2 = FFT_N2
    N1h = L // N2
    N1 = 2 * N1h
    N = N1 * N2
    k1 = jnp.arange(N1, dtype=i32)
    n1 = jnp.arange(N1, dtype=i32)
    n2 = jnp.arange(N2, dtype=i32)
    ph = (k1[None, :, None] * (n1[None, None, :] * N2 + n2[:, None, None])) % N
    ang = ph.astype(f32) * (2.0 * math.pi / N)
    gr, gi = jnp.cos(ang), -jnp.sin(ang)
    grh, gih = gr[:, :, :N1h], gi[:, :, :N1h]
    g_fwd = jnp.concatenate([jnp.concatenate([grh, -gih], axis=2),
                             jnp.concatenate([gih, grh], axis=2)], axis=1).astype(bf16)
    g_real = jnp.concatenate([gr, gi], axis=1).astype(bf16)
    mr = jnp.swapaxes(grh, 1, 2)
    mi = -jnp.swapaxes(gih, 1, 2)
    g_inv = jnp.concatenate([jnp.concatenate([mr, -mi], axis=2),
                             jnp.concatenate([mi, mr], axis=2)], axis=1).astype(bf16)
    kk = jnp.arange(N2, dtype=i32)
    a2 = ((kk[:, None] * kk[None, :]) % N2).astype(f32) * (2.0 * math.pi / N2)
    fr, fi = jnp.cos(a2), -jnp.sin(a2)
    fb_fwd = jnp.concatenate([jnp.concatenate([fr, -fi], axis=1),
                              jnp.concatenate([fi, fr], axis=1)], axis=0).astype(bf16)
    fb_inv = jnp.concatenate([jnp.concatenate([fr, fi], axis=1),
                              jnp.concatenate([-fi, fr], axis=1)], axis=0).astype(bf16)
    return dict(N1h=N1h, N1=N1, N2=N2, N=N, g_fwd=g_fwd, g_real=g_real, g_inv=g_inv, fb_fwd=fb_fwd, fb_inv=fb_inv)


def _strided_rows(refs, start, count, stride):
    parts = []
    for ref in refs:
        assert ref.shape[-1] == LANES
        flat = ref.reshape(math.prod(ref.shape[:-1]), LANES)
        parts.append(flat[pl.ds(start, count, stride=stride), :])
    return parts[0] if len(parts) == 1 else jnp.concatenate(parts, axis=1)


def _pick(refs, j):
    n = refs[0].shape[0]
    if refs[0].shape[-1] == LANES:
        return jnp.concatenate([_strided_rows(refs, j, n, 16), _strided_rows(refs, 8 + j, n, 16)], axis=0)
    (ref,) = refs
    return jnp.concatenate([ref[:, 0, j, :], ref[:, 1, j, :]], axis=0)


def _filt_a_kernel(z_ref, fw1, fb1, fw2, fb2, fw3, fb3, wo0, wo1, freq, dl, g_ref, y_ref, l1_ref, *, N1h):
    @pl.when(pl.program_id(0) == 0)
    def _():
        l1_ref[...] = jnp.zeros_like(l1_ref)

    for q in range(z_ref.shape[0]):
        taps = _filter_taps(z_ref[q], fw1[...], fb1[...], fw2[...], fb2[...], fw3[...], fb3[...],
                            wo0[...], wo1[...], freq[...], dl[...], N1h)
        l1_ref[...] += jnp.sum(jnp.abs(taps), axis=0, keepdims=True)
        y_ref[q] = jnp.dot(g_ref[q], taps.astype(bf16), preferred_element_type=f32)


def _filt_b_kernel(ya_ref, yb_ref, fb_ref, l1_ref, k_ref, *, N):
    scale = 1.0 / (l1_ref[...] * float(N))
    for j in range(8):
        kf = jnp.dot(fb_ref[...], _pick((ya_ref, yb_ref), j).astype(bf16), preferred_element_type=f32) * scale
        k_ref[j] = kf.astype(k_ref.dtype)


def _latent_filter(ztab, fl, dc, h):
    fw1, fb1, fw2, fb2, fw3, fb3, fwout, freq, deltas = fl
    N1h, N1, N2, N = dc["N1h"], dc["N1"], dc["N2"], dc["N"]
    full = lambda a: pl.BlockSpec(a.shape, lambda n: (0,) * a.ndim)
    yk, l1 = pl.pallas_call(
        functools.partial(_filt_a_kernel, N1h=N1h),
        out_shape=(jax.ShapeDtypeStruct((N2, 2 * N1, 2 * h), f32), jax.ShapeDtypeStruct((1, 2 * h), f32)),
        grid=(N2 // FILT_Q,),
        in_specs=[pl.BlockSpec((FILT_Q, N1, EMB_PAD), lambda n: (n, 0, 0)),
                  full(fw1), full(fb1), full(fw2), full(fb2), full(fw3), full(fb3),
                  pl.BlockSpec((fwout.shape[0], 2 * h), lambda n: (0, 0)),
                  pl.BlockSpec((fwout.shape[0], 2 * h), lambda n: (0, 1)),
                  full(freq), full(deltas),
                  pl.BlockSpec((FILT_Q, 2 * N1, N1), lambda n: (n, 0, 0))],
        out_specs=[pl.BlockSpec((FILT_Q, 2 * N1, 2 * h), lambda n: (n, 0, 0)),
                   pl.BlockSpec((1, 2 * h), lambda n: (0, 0))],
        compiler_params=_cp(("arbitrary",)),
        name="filter_stage_a",
    )(ztab, fw1, fb1, fw2, fb2, fw3, fb3, fwout, fwout, freq, deltas, dc["g_real"])
    cb = 2 * LANES
    yk4 = yk.reshape(N2, 2, N1, 2 * h)
    return pl.pallas_call(
        functools.partial(_filt_b_kernel, N=N),
        out_shape=jax.ShapeDtypeStruct((N1, 2 * N2, 2 * h), bf16),
        grid=(N1 // 8, (2 * h) // cb),
        in_specs=[pl.BlockSpec((N2, 2, 8, LANES), lambda i, c: (0, 0, i, 2 * c)),
                  pl.BlockSpec((N2, 2, 8, LANES), lambda i, c: (0, 0, i, 2 * c + 1)),
                  pl.BlockSpec((2 * N2, 2 * N2), lambda i, c: (0, 0)),
                  pl.BlockSpec((1, cb), lambda i, c: (0, c))],
        out_specs=pl.BlockSpec((8, 2 * N2, cb), lambda i, c: (i, 0, c)),
        compiler_params=_cp(("arbitrary", "arbitrary")),
        name="filter_stage_b",
    )(yk4, yk4, dc["fb_fwd"], l1)


def _stack_batches(ref, j):
    if ref.shape[-1] == LANES:
        return _strided_rows((ref,), j, ref.shape[0] * ref.shape[1], 8)
    return jnp.concatenate([ref[0, :, j, :], ref[1, :, j, :]], axis=0)


def _conv_a_kernel(u_ref, g_ref, y_ref):
    for j in range(8):
        y_ref[j] = jnp.dot(g_ref[j], _stack_batches(u_ref, j).astype(bf16), preferred_element_type=f32)


def _conv_b_kernel(ya_ref, yb_ref, fbf_ref, fbi_ref, k_ref, w_ref, *, N2):
    for j in range(8):
        z = jnp.dot(fbf_ref[...], _pick((ya_ref, yb_ref), j).astype(bf16), preferred_element_type=f32)
        zr, zi = z[:N2], z[N2:]
        kr, ki = k_ref[j, :N2, :].astype(f32), k_ref[j, N2:, :].astype(f32)
        p = jnp.concatenate([zr * kr - zi * ki, zr * ki + zi * kr], axis=0).astype(bf16)
        w_ref[j] = jnp.dot(fbi_ref[...], p, preferred_element_type=f32)


def _conv_mid_kernel(w_ref, gi_ref, g_ref, v_ref, x1_ref, b_ref, z_ref, y_ref):
    for j in range(8):
        y = jnp.dot(gi_ref[j], _pick((w_ref,), j).astype(bf16), preferred_element_type=f32)
        vv = _stack_batches(v_ref, j)
        z = _stack_batches(x1_ref, j) * (y + b_ref[...] * vv)
        z_ref[j] = z
        y_ref[j] = jnp.dot(g_ref[j], z.astype(bf16), preferred_element_type=f32)


def _store_strided_rows(ref, start, stride, val):
    assert ref.shape[-1] == LANES
    flat = ref.reshape(math.prod(ref.shape[:-1]), LANES)
    flat[pl.ds(start, val.shape[0], stride=stride), :] = val


def _conv_out_kernel(w_ref, gi_ref, z_ref, x2_ref, b_ref, o_ref):
    for j in range(8):
        y = jnp.dot(gi_ref[j], _pick((w_ref,), j).astype(bf16), preferred_element_type=f32)
        o = _stack_batches(x2_ref, j) * (y + b_ref[...] * z_ref[j])
        if o_ref.shape[-1] == LANES:
            _store_strided_rows(o_ref, j, 8, o)
        else:
            nh = o_ref.shape[1]
            o_ref[0, :, j, :] = o[:nh]
            o_ref[1, :, j, :] = o[nh:]


def _latent_hyena(u_lat, kf, bias, dc, B, L):
    assert B == 2, "the long convolution packs exactly two batch rows into one complex sequence"
    h = u_lat.shape[1] // 3
    N1h, N1, N2 = dc["N1h"], dc["N1"], dc["N2"]
    u4 = u_lat.reshape(B, N1h, N2, 3 * h)
    cb = min(512, h)
    nb = h // cb
    cb2 = 2 * LANES
    nb2 = h // cb2
    bias2 = bias.reshape(1, 2 * h)
    ublk = lambda col: pl.BlockSpec((B, N1h, 8, cb), lambda i, c: (0, 0, i, col * nb + c))

    ya = pl.pallas_call(
        _conv_a_kernel,
        out_shape=jax.ShapeDtypeStruct((N2, 2 * N1, h), f32),
        grid=(N2 // 8, nb),
        in_specs=[ublk(0), pl.BlockSpec((8, 2 * N1, 2 * N1h), lambda i, c: (i, 0, 0))],
        out_specs=pl.BlockSpec((8, 2 * N1, cb), lambda i, c: (i, 0, c)),
        compiler_params=_cp(("arbitrary", "arbitrary")),
        name="conv_stage_a",
    )(u4, dc["g_fwd"])

    def stage_b(y, order):
        y4 = y.reshape(N2, 2, N1, h)
        return pl.pallas_call(
            functools.partial(_conv_b_kernel, N2=N2),
            out_shape=jax.ShapeDtypeStruct((N1, 2 * N2, h), f32),
            grid=(N1 // 8, nb2),
            in_specs=[pl.BlockSpec((N2, 2, 8, LANES), lambda i, c: (0, 0, i, 2 * c)),
                      pl.BlockSpec((N2, 2, 8, LANES), lambda i, c: (0, 0, i, 2 * c + 1)),
                      pl.BlockSpec((2 * N2, 2 * N2), lambda i, c: (0, 0)),
                      pl.BlockSpec((2 * N2, 2 * N2), lambda i, c: (0, 0)),
                      pl.BlockSpec((8, 2 * N2, cb2), lambda i, c: (i, 0, order * nb2 + c))],
            out_specs=pl.BlockSpec((8, 2 * N2, cb2), lambda i, c: (i, 0, c)),
            compiler_params=_cp(("arbitrary", "arbitrary")),
            name="conv_stage_b",
        )(y4, y4, dc["fb_fwd"], dc["fb_inv"], kf)

    w1 = stage_b(ya, 0)
    zp, ya2 = pl.pallas_call(
        _conv_mid_kernel,
        out_shape=(jax.ShapeDtypeStruct((N2, 2 * N1h, h), f32), jax.ShapeDtypeStruct((N2, 2 * N1, h), f32)),
        grid=(N2 // 8, nb),
        in_specs=[pl.BlockSpec((N1, 2, 8, cb), lambda i, c: (0, 0, i, c)),
                  pl.BlockSpec((8, 2 * N1h, 2 * N1), lambda i, c: (i, 0, 0)),
                  pl.BlockSpec((8, 2 * N1, 2 * N1h), lambda i, c: (i, 0, 0)),
                  ublk(0), ublk(1),
                  pl.BlockSpec((1, cb), lambda i, c: (0, c))],
        out_specs=[pl.BlockSpec((8, 2 * N1h, cb), lambda i, c: (i, 0, c)),
                   pl.BlockSpec((8, 2 * N1, cb), lambda i, c: (i, 0, c))],
        compiler_params=_cp(("arbitrary", "arbitrary")),
        name="conv_stage_mid",
    )(w1.reshape(N1, 2, N2, h), dc["g_inv"], dc["g_fwd"], u4, u4, bias2)
    w2 = stage_b(ya2, 1)
    y = pl.pallas_call(
        _conv_out_kernel,
        out_shape=jax.ShapeDtypeStruct((B, N1h, N2, h), f32),
        grid=(N2 // 8, nb),
        in_specs=[pl.BlockSpec((N1, 2, 8, cb), lambda i, c: (0, 0, i, c)),
                  pl.BlockSpec((8, 2 * N1h, 2 * N1), lambda i, c: (i, 0, 0)),
                  pl.BlockSpec((8, 2 * N1h, cb), lambda i, c: (i, 0, c)),
                  ublk(2),
                  pl.BlockSpec((1, cb), lambda i, c: (0, nb + c))],
        out_specs=pl.BlockSpec((B, N1h, 8, cb), lambda i, c: (0, 0, i, c)),
        compiler_params=_cp(("arbitrary", "arbitrary")),
        name="conv_stage_out",
    )(w2.reshape(N1, 2, N2, h), dc["g_inv"], zp, u4, bias2)
    return y.reshape(B * L, h)


def _merge_kernel(of_ref, ob_ref, og_ref, ng_ref, yl_ref, yc_ref, gt_hy_ref, gt_hg_ref, phy_ref, phg_ref, u_ref,
                  *, H, nlat):
    i = pl.program_id(0)
    o = of_ref[...].astype(f32) + ob_ref[...].astype(f32)
    og = og_ref[...].astype(f32)
    parts = []
    for h in range(H):
        hs = slice(h * HEAD_DIM, (h + 1) * HEAD_DIM)
        oh = o[:, hs]
        r = lax.rsqrt(jnp.mean(oh * oh, axis=-1, keepdims=True) + RMS_EPS)
        parts.append((oh * r * ng_ref[...] * og[:, hs]).astype(bf16))
    y_hg = jnp.concatenate(parts, axis=1)
    y_hy = jnp.where(i < nlat, yl_ref[...], yc_ref[...]).astype(bf16)
    a = jnp.dot(y_hy, phy_ref[...], preferred_element_type=f32)
    b = jnp.dot(y_hg, phg_ref[...], preferred_element_type=f32)
    u_ref[...] = (gt_hy_ref[...].astype(f32) * a + gt_hg_ref[...].astype(f32) * b).astype(u_ref.dtype)


def _merge(o_f, o_b, hg, ng, y_lat, y_ctx, gates, phy, phg, layer, tm, ntile):
    h = o_f.shape[1]
    D = phy.shape[2]
    H = h // HEAD_DIM
    nlat = y_lat.shape[0] // tm
    nctx = y_ctx.shape[0] // tm
    return pl.pallas_call(
        functools.partial(_merge_kernel, H=H, nlat=nlat),
        out_shape=jax.ShapeDtypeStruct((ntile * tm, D), bf16),
        grid=(ntile,),
        in_specs=[
            pl.BlockSpec((tm, h), lambda i: (i, 0)),
            pl.BlockSpec((tm, h), lambda i: (i, 0)),
            pl.BlockSpec((tm, h), lambda i: (i, COL_OG)),
            pl.BlockSpec((1, HEAD_DIM), lambda i: (0, 0)),
            pl.BlockSpec((tm, h), lambda i: (jnp.minimum(i, nlat - 1), 0)),
            pl.BlockSpec((tm, h), lambda i: (jnp.clip(i - nlat, 0, nctx - 1), 0)),
            pl.BlockSpec((tm, D), lambda i: (i, COL_GATE // 2)),
            pl.BlockSpec((tm, D), lambda i: (i, COL_GATE // 2 + 1)),
            pl.BlockSpec((None, h, D), lambda i: (layer, 0, 0)),
            pl.BlockSpec((None, h, D), lambda i: (layer, 0, 0)),
        ],
        out_specs=pl.BlockSpec((tm, D), lambda i: (i, 0)),
        compiler_params=_cp(("arbitrary",)),
        name="merge",
    )(o_f, o_b, hg, ng, y_lat, y_ctx, gates, gates, phy, phg)


def _norm_mod(x, g, shift, scale):
    ms = jnp.mean(x * x, axis=-1, keepdims=True)
    return (x * lax.rsqrt(ms + RMS_EPS) * g) * (1.0 + scale) + shift


def _outproj_kernel(seg_ref, u_ref, w_ref, x_ref, mod_ref, g2_ref, rh_ref, rl_ref, xo_ref, lo_ref, *, D):
    seg = seg_ref[pl.program_id(0)]
    mix = jnp.dot(u_ref[...], w_ref[...], preferred_element_type=f32)
    gate = mod_ref[pl.ds(seg, 1), 2 * D:3 * D]
    xn = x_ref[...] + gate * mix
    xo_ref[...] = xn
    t2 = _norm_mod(xn, g2_ref[...], mod_ref[pl.ds(seg, 1), 3 * D:4 * D], mod_ref[pl.ds(seg, 1), 4 * D:5 * D])
    th = t2.astype(bf16)
    tl = (t2 - th.astype(f32)).astype(bf16)
    lo_ref[...] = (lax.dot_general(rh_ref[...], th, _NT, preferred_element_type=f32)
                   + lax.dot_general(rh_ref[...], tl, _NT, preferred_element_type=f32)
                   + lax.dot_general(rl_ref[...], th, _NT, preferred_element_type=f32))


def _outproj(u, w_out, layer, xs, tile_seg, mod_l, g2, rh, rl, tm, ntile):
    D = w_out.shape[1]
    E = rh.shape[0]
    return pl.pallas_call(
        functools.partial(_outproj_kernel, D=D),
        out_shape=(jax.ShapeDtypeStruct((ntile * tm, D), f32), jax.ShapeDtypeStruct((E, ntile * tm), f32)),
        grid_spec=pltpu.PrefetchScalarGridSpec(
            num_scalar_prefetch=1,
            grid=(ntile,),
            in_specs=[
                pl.BlockSpec((tm, D), lambda i, s: (i, 0)),
                pl.BlockSpec((None, D, D), lambda i, s: (layer, 0, 0)),
                pl.BlockSpec((tm, D), lambda i, s: (i, 0)),
                pl.BlockSpec((8, 6 * D), lambda i, s: (0, 0)),
                pl.BlockSpec((1, D), lambda i, s: (0, 0)),
                pl.BlockSpec((E, D), lambda i, s: (0, 0)),
                pl.BlockSpec((E, D), lambda i, s: (0, 0)),
            ],
            out_specs=[pl.BlockSpec((tm, D), lambda i, s: (i, 0)),
                       pl.BlockSpec((E, tm), lambda i, s: (0, i))],
        ),
        compiler_params=_cp(("arbitrary",)),
        name="outproj",
    )(tile_seg, u, w_out, xs, mod_l, g2, rh, rl)


_PAIRS = ((0, 1), (0, 2), (0, 3), (1, 2), (1, 3), (2, 3))


def _route_kernel(lo_ref, rb_ref, o_ref, *, E):
    per = E // N_GROUPS
    assert per == 4
    lo = lo_ref[...]
    sc = jax.nn.sigmoid(lo)
    sel = sc + rb_ref[...]
    srow = [sel[e:e + 1, :] for e in range(E)]
    crow = [sc[e:e + 1, :] for e in range(E)]
    gs = []
    for g in range(N_GROUPS):
        x = srow[per * g:per * g + per]
        m = x[0] + x[1]
        for (a, b) in _PAIRS[1:]:
            m = jnp.maximum(m, x[a] + x[b])
        gs.append(m)
    gbest = jnp.zeros_like(gs[0]).astype(i32)
    best = gs[0]
    for g in range(1, N_GROUPS):
        better = gs[g] > best
        gbest = jnp.where(better, g, gbest)
        best = jnp.where(better, gs[g], best)

    def pick(rows, i):
        out = rows[i]
        for g in range(1, N_GROUPS):
            out = jnp.where(gbest == g, rows[per * g + i], out)
        return out

    x = [pick(srow, i) for i in range(per)]
    s = [pick(crow, i) for i in range(per)]
    chosen = []
    for i in range(per):
        cnt = jnp.zeros_like(gbest)
        for j in range(per):
            if j == i:
                continue
            beats = (x[j] >= x[i]) if j < i else (x[j] > x[i])
            cnt = cnt + jnp.where(beats, 1, 0)
        chosen.append(cnt < 2)
    pair = jnp.zeros_like(gbest)
    wa = jnp.zeros_like(best)
    wb = jnp.zeros_like(best)
    for p, (a, b) in enumerate(_PAIRS):
        hit = jnp.where(chosen[a], jnp.where(chosen[b], 1, 0), 0) == 1
        pair = jnp.where(hit, p, pair)
        wa = jnp.where(hit, s[a], wa)
        wb = jnp.where(hit, s[b], wb)
    tot = wa + wb
    cls = (gbest * len(_PAIRS) + pair).astype(f32)
    o_ref[...] = jnp.concatenate([cls, wa / tot, wb / tot, jnp.zeros((5, cls.shape[1]), f32)], axis=0)


def _route(logits_t, router_b, tl):
    E, T = logits_t.shape
    return pl.pallas_call(
        functools.partial(_route_kernel, E=E),
        out_shape=jax.ShapeDtypeStruct((8, T), f32),
        grid=(T // tl,),
        in_specs=[pl.BlockSpec((E, tl), lambda i: (0, i)), pl.BlockSpec((E, 1), lambda i: (0, 0))],
        out_specs=pl.BlockSpec((8, tl), lambda i: (0, i)),
        compiler_params=_cp(("arbitrary",)),
        name="route",
    )(logits_t, router_b.reshape(E, 1))


def _moe_kernel(be_ref, nb_ref, x_ref, rw_ref, mod_ref, g2_ref, w1_ref, w3_ref, w2_ref, o_ref, h_scr, acc_scr, *, D, nseg):
    i = pl.program_id(0)
    k = pl.program_id(1)
    f = pl.program_id(2)
    nf = pl.num_programs(2)
    live = i < nb_ref[0]

    def seg_rows(col0):
        seg = rw_ref[:, 2:3]
        out = mod_ref[0:1, col0:col0 + D]
        for r in range(1, nseg):
            out = jnp.where(seg == float(r), mod_ref[r:r + 1, col0:col0 + D], out)
        return out

    @pl.when(live & (k == 0) & (f == 0))
    def _():
        h_scr[...] = _norm_mod(x_ref[...], g2_ref[...], seg_rows(3 * D), seg_rows(4 * D)).astype(bf16)
        acc_scr[...] = jnp.zeros_like(acc_scr)

    @pl.when(live)
    def _():
        hx = h_scr[...]
        a = jnp.dot(hx, w1_ref[...], preferred_element_type=f32)
        b = jnp.dot(hx, w3_ref[...], preferred_element_type=f32)
        wsel = jnp.where(k == 0, rw_ref[:, 0:1], rw_ref[:, 1:2])
        act = (_silu(a) * b * wsel).astype(bf16)
        acc_scr[...] += jnp.dot(act, w2_ref[...], preferred_element_type=f32)

    @pl.when(live & (k == 1) & (f == nf - 1))
    def _():
        o_ref[...] = x_ref[...] + seg_rows(5 * D) * acc_scr[...]

    @pl.when(jnp.logical_not(live) & (k == 1) & (f == nf - 1))
    def _():
        o_ref[...] = x_ref[...]


def _moe(xg, rw, blk_e, nblk_used, mod_l, g2, w1, w3, w2, layer, nseg):
    P, D = xg.shape
    F = w1.shape[3]
    tf = min(512, F)
    nblk = P // MOE_ROWS
    nf = F // tf

    def wsel(i, k, f, be, nb):
        live = i < nb[0]
        e = be[2 * jnp.minimum(i, nb[0] - 1) + jnp.where(live, k, 1)]
        return e, jnp.where(live, f, nf - 1)

    def w13(i, k, f, be, nb):
        e, ff = wsel(i, k, f, be, nb)
        return (layer, e, 0, ff)

    def w2m(i, k, f, be, nb):
        e, ff = wsel(i, k, f, be, nb)
        return (layer, e, ff, 0)

    return pl.pallas_call(
        functools.partial(_moe_kernel, D=D, nseg=nseg),
        out_shape=jax.ShapeDtypeStruct((P, D), f32),
        grid_spec=pltpu.PrefetchScalarGridSpec(
            num_scalar_prefetch=2,
            grid=(nblk, 2, nf),
            in_specs=[
                pl.BlockSpec((MOE_ROWS, D), lambda i, k, f, be, nb: (i, 0)),
                pl.BlockSpec((MOE_ROWS, 128), lambda i, k, f, be, nb: (i, 0)),
                pl.BlockSpec((8, 6 * D), lambda i, k, f, be, nb: (0, 0)),
                pl.BlockSpec((1, D), lambda i, k, f, be, nb: (0, 0)),
                pl.BlockSpec((None, None, D, tf), w13),
                pl.BlockSpec((None, None, D, tf), w13),
                pl.BlockSpec((None, None, tf, D), w2m),
            ],
            out_specs=pl.BlockSpec((MOE_ROWS, D), lambda i, k, f, be, nb: (i, 0)),
            scratch_shapes=[pltpu.VMEM((MOE_ROWS, D), bf16), pltpu.VMEM((MOE_ROWS, D), f32)],
        ),
        compiler_params=_cp(("arbitrary", "arbitrary", "arbitrary")),
        name="moe_experts",
    )(blk_e, nblk_used, xg, rw, mod_l, g2, w1, w3, w2)


def _final_norm_kernel(x_ref, g_ref, o_ref):
    x = x_ref[...]
    ms = jnp.mean(x * x, axis=-1, keepdims=True)
    o_ref[...] = x * lax.rsqrt(ms + RMS_EPS) * g_ref[...]


def _final_norm(x, g, tm):
    n, D = x.shape
    return pl.pallas_call(
        _final_norm_kernel,
        out_shape=jax.ShapeDtypeStruct((n, D), f32),
        grid=(n // tm,),
        in_specs=[pl.BlockSpec((tm, D), lambda i: (i, 0)), pl.BlockSpec((1, D), lambda i: (0, 0))],
        out_specs=pl.BlockSpec((tm, D), lambda i: (i, 0)),
        compiler_params=_cp(("arbitrary",)),
        name="final_norm",
    )(x, g)


def _feature_rows(pos, valid, Lx, bands):
    t = jnp.linspace(0.0, 1.0, Lx, dtype=f32)[jnp.clip(pos, 0, Lx - 1)][..., None]
    w = ((2.0 * math.pi / Lx) * jnp.clip(pos, 0, Lx - 1).astype(f32))[..., None]
    fb = jnp.linspace(1e-4, bands - 1, bands, dtype=f32)
    z = jnp.concatenate([t, jnp.cos(fb * w), -jnp.sin(fb * w)], axis=-1)
    z = jnp.pad(z, [(0, 0)] * (z.ndim - 1) + [(0, EMB_PAD - 1 - z.shape[-1])])
    return jnp.concatenate([z, valid.astype(f32)[..., None]], axis=-1)


def _latent_ztab(L, bands, dc):
    N1, N2 = dc["N1"], dc["N2"]
    tau = jnp.arange(N1, dtype=i32)[None, :] * N2 + jnp.arange(N2, dtype=i32)[:, None]
    pos = jnp.where(tau < L, tau, 2 * L - tau)
    return _feature_rows(pos, tau != L, L, bands)


def _ctx_ztab(Lc, bands):
    r = jnp.arange(2 * Lc, dtype=i32)
    return _feature_rows(jnp.abs(r - Lc), r != 0, Lc, bands)


def kernel(x, c, ctx, c_ctx, ada_w, ada_b, norm1_g, norm2_g, final_g, w_in, hy_conv_w, hy_conv_b, hy_fw1, hy_fb1,
           hy_fw2, hy_fb2, hy_fw3, hy_fb3, hy_fwout, hy_freq, hy_bias, hg_lb_raw, hg_norm_g, p_hy, p_hg, w_out,
           router_w, router_b, moe_w1, moe_w3, moe_w2):
    B, L, D = x.shape
    Lc = ctx.shape[1]
    depth = ada_w.shape[0]
    h = D // 2
    E = router_w.shape[1]
    emb = hy_fw1.shape[1]
    bands = (emb - 1) // 2
    BL, BLc = B * L, B * Lc
    T = BL + BLc
    tm = BLc
    assert L % tm == 0 and L % FFT_N2 == 0 and Lc % SCAN_CHUNK == 0 and h % HEAD_DIM == 0 and B + 1 <= 8
    nlat = BL // tm
    ntile = T // tm
    tile_seg = jnp.concatenate([jnp.repeat(jnp.arange(B, dtype=i32), L // tm), jnp.full((1,), B, i32)])
    tmi = 2 * tm if L % (2 * tm) == 0 else tm
    T_pad = (T + tmi - 1) // tmi * tmi
    seg_in = jnp.minimum((jnp.arange(T_pad // tmi, dtype=i32) * tmi) // L, B)

    xs = jnp.concatenate([x.reshape(BL, D).astype(f32), ctx.reshape(BLc, D).astype(f32),
                          jnp.zeros((T_pad - T, D), f32)], axis=0)
    cond = jnp.concatenate([c.astype(f32), c_ctx.astype(f32)[None, :]], axis=0)
    mod = _modulation(cond, ada_w.astype(f32), ada_b.astype(f32))

    lb = jnp.cumsum(jax.nn.softmax(hg_lb_raw.astype(f32), axis=0), axis=0)
    lb = lb - lb[:1]

    dc = _dft_constants(L)
    zt_lat = _latent_ztab(L, bands, dc)
    zt_ctx = _ctx_ztab(Lc, bands)
    deltas = jnp.abs(jnp.linspace(HY_MIN_DECAY, HY_MAX_DECAY, 2 * h, dtype=f32)).reshape(1, 2 * h)

    w_in_b, p_hy_b, p_hg_b, w_out_b = (_to_bf16(w.astype(f32)) for w in (w_in, p_hy, p_hg, w_out))
    moe_w1_b, moe_w3_b, moe_w2_b = (_to_bf16(w.astype(f32)) for w in (moe_w1, moe_w3, moe_w2))

    rw_t = router_w.astype(f32).T
    rh = rw_t.astype(bf16)
    rl = (rw_t - rh.astype(f32)).astype(bf16)
    n_cls = N_GROUPS * len(_PAIRS)
    per = E // N_GROUPS
    cls_e = jnp.array([[per * g + a, per * g + b] for g in range(N_GROUPS) for (a, b) in _PAIRS], i32)

    for l in range(depth):
        last = l == depth - 1
        mod_l = mod[l]
        proj, lg = _inproj(xs, seg_in, mod_l, norm1_g[l].astype(f32).reshape(1, D), w_in_b, l, lb[l], tmi)
        p_hyena = hg = gates = proj
        o_f, o_b = _hgrn_scan(hg, lg, B, L, Lc)

        pad = EMB_PAD - emb
        wo = hy_fwout[l].astype(f32)
        wo_hi = wo.astype(bf16)
        wo_lo = (wo - wo_hi.astype(f32)).astype(bf16)
        fl = (jnp.pad(hy_fw1[l].astype(f32), ((0, pad), (0, 0))), hy_fb1[l].astype(f32).reshape(1, -1),
              hy_fw2[l].astype(f32), hy_fb2[l].astype(f32).reshape(1, -1),
              hy_fw3[l].astype(f32), hy_fb3[l].astype(f32).reshape(1, -1),
              jnp.concatenate([wo_hi, wo_hi, wo_lo], axis=0), hy_freq[l].astype(f32), deltas)
        u_lat, u_ctx = _shortconv(p_hyena, hy_conv_w[l].astype(f32), hy_conv_b[l].astype(f32), B, L, Lc)
        kf = _latent_filter(zt_lat, fl, dc, h)
        y_lat = _latent_hyena(u_lat, kf, hy_bias[l].astype(f32), dc, B, L)
        if last:
            y_ctx = jnp.zeros((BLc, h), f32)
            nt = nlat
        else:
            kk = _ctx_filter(zt_ctx, fl, Lc, h)
            y_ctx = _ctx_conv(u_ctx, kk, hy_bias[l].astype(f32), B, Lc)
            nt = ntile
        u = _merge(o_f, o_b, hg, hg_norm_g[l].astype(f32).reshape(1, HEAD_DIM), y_lat, y_ctx, gates,
                   p_hy_b, p_hg_b, l, tm, nt)
        g2 = norm2_g[l].astype(f32).reshape(1, D)
        x1, logits_t = _outproj(u, w_out_b, l, xs, tile_seg, mod_l, g2, rh, rl, tm, nt)

        Tm = nt * tm
        route = _route(logits_t, router_b.astype(f32), tm)
        cls = route[0].astype(i32)
        counts = jnp.sum((cls[:, None] == jnp.arange(n_cls, dtype=i32)[None, :]).astype(i32), axis=0)
        padded = (counts + MOE_ROWS - 1) // MOE_ROWS * MOE_ROWS
        pend = jnp.cumsum(padded)
        pstart = pend - padded
        sstart = jnp.cumsum(counts) - counts
        order = jnp.argsort(cls, stable=True).astype(i32)
        rank_of_tok = jnp.argsort(order).astype(i32)
        pos_of_tok = pstart[cls] + rank_of_tok - sstart[cls]
        P = (Tm + n_cls * (MOE_ROWS - 1) + MOE_ROWS - 1) // MOE_ROWS * MOE_ROWS
        nblk = P // MOE_ROWS
        blk_start = jnp.arange(nblk, dtype=i32) * MOE_ROWS
        blk_cls = jnp.minimum(jnp.sum((blk_start[:, None] >= pend[None, :]).astype(i32), axis=1), n_cls - 1)
        blk_e = cls_e[blk_cls].reshape(-1).astype(i32)
        nblk_used = (pend[-1] // MOE_ROWS).astype(i32).reshape(1)
        row_cls = jnp.repeat(blk_cls, MOE_ROWS)
        within = jnp.arange(P, dtype=i32) - pstart[row_cls]
        okf = (within < counts[row_cls]).astype(f32)
        src = order[jnp.clip(sstart[row_cls] + within, 0, Tm - 1)]
        seg_tok = jnp.repeat(tile_seg[:nt], tm).astype(f32)
        rw = jnp.stack([route[1][src] * okf, route[2][src] * okf, seg_tok[src]], axis=1)
        rw = jnp.pad(rw, ((0, 0), (0, 128 - 3)))
        xg = x1[src]
        yg = _moe(xg, rw, blk_e, nblk_used, mod_l, g2, moe_w1_b, moe_w3_b, moe_w2_b, l, B + 1)
        if last:
            xs = yg[pos_of_tok]
        else:
            xs = yg[jnp.concatenate([pos_of_tok, jnp.zeros((T_pad - T,), i32)])]

    out = _final_norm(xs[:BL], final_g.astype(f32).reshape(1, D), tm)
    return out.reshape(B, L, D).astype(x.dtype)
```

```python
import functools
import math

import jax
import jax.numpy as jnp
from jax import lax
from jax.experimental import pallas as pl
from jax.experimental.pallas import tpu as pltpu

f32 = jnp.float32
bf16 = jnp.bfloat16
i32 = jnp.int32

RMS_EPS = 1e-6
MXU_COLS = 256
LANES = 128
HEAD_DIM = 128
SCAN_CHUNK = 64
LOG2_E = 1.4426950408889634
DIAG_EXP2_CAP = 120.0
N_GROUPS = 4
HY_MAX_DECAY = math.log(1e-2) / 0.3
HY_MIN_DECAY = math.log(1e-2) / 1.5
FFT_N2 = 256
EMB_PAD = 64
FILT_Q = 2
MOE_ROWS = 512
VMEM_LIMIT = 52 * 1024 * 1024
CAST_BLOCK_BYTES = 8 * 1024 * 1024
HIGHEST = lax.Precision.HIGHEST

_NT = (((1,), (1,)), ((), ()))
_TN = (((0,), (0,)), ((), ()))


def _cp(sem, vmem=VMEM_LIMIT):
    return pltpu.CompilerParams(dimension_semantics=sem, vmem_limit_bytes=vmem)


def _silu(x):
    return x * jax.nn.sigmoid(x)


def _sigmoid_t(x):
    return 0.5 * jnp.tanh(0.5 * x) + 0.5


def _cast_kernel(x_ref, o_ref):
    o_ref[...] = x_ref[...].astype(o_ref.dtype)


def _to_bf16(w):
    shape = w.shape
    cols = shape[-1]
    rows = math.prod(shape[:-1])
    tr = 16
    while tr * 2 * cols * 4 <= CAST_BLOCK_BYTES and rows % (tr * 2) == 0:
        tr *= 2
    assert rows % tr == 0
    out = pl.pallas_call(
        _cast_kernel,
        out_shape=jax.ShapeDtypeStruct((rows, cols), bf16),
        grid=(rows // tr,),
        in_specs=[pl.BlockSpec((tr, cols), lambda i: (i, 0))],
        out_specs=pl.BlockSpec((tr, cols), lambda i: (i, 0)),
        compiler_params=_cp(("arbitrary",)),
        name="cast_bf16",
    )(w.reshape(rows, cols))
    return out.reshape(shape)


def _mod_kernel(sb_ref, w_ref, b_ref, o_ref, *, nrow, D, tn):
    rep = tn // 128

    def body(i, accs):
        k0 = pl.multiple_of(i * 8, 8)
        w = w_ref[pl.ds(k0, 8), :]
        out = []
        for m in range(nrow):
            sb = sb_ref[m, pl.ds(k0, 8), :]
            out.append(accs[m] + w * jnp.concatenate([sb] * rep, axis=1))
        return tuple(out)

    accs = lax.fori_loop(0, D // 8, body, tuple(jnp.zeros((8, tn), f32) for _ in range(nrow)), unroll=4)
    o_ref[...] = jnp.zeros_like(o_ref)
    for m in range(nrow):
        o_ref[m:m + 1, :] = jnp.sum(accs[m], axis=0, keepdims=True) + b_ref[...]


def _modulation(cond, ada_w, ada_b):
    nrow, D = cond.shape
    depth, _, n6 = ada_w.shape
    tn = 1024 if n6 % 1024 == 0 else n6
    sb = jnp.broadcast_to(_silu(cond)[:, :, None], (nrow, D, 128))
    return pl.pallas_call(
        functools.partial(_mod_kernel, nrow=nrow, D=D, tn=tn),
        out_shape=jax.ShapeDtypeStruct((depth, 8, n6), f32),
        grid=(depth, n6 // tn),
        in_specs=[
            pl.BlockSpec((nrow, D, 128), lambda l, j: (0, 0, 0)),
            pl.BlockSpec((None, D, tn), lambda l, j: (l, 0, j)),
            pl.BlockSpec((None, 1, tn), lambda l, j: (l, 0, j)),
        ],
        out_specs=pl.BlockSpec((None, 8, tn), lambda l, j: (l, 0, j)),
        compiler_params=_cp(("arbitrary", "arbitrary")),
        name="modulation",
    )(sb, ada_w, ada_b.reshape(depth, 1, n6))


def _inproj_kernel(seg_ref, x_ref, mod_ref, g_ref, w_ref, lb_ref, p_ref, lg_ref, hx_scr, *, D):
    i = pl.program_id(0)
    j = pl.program_id(1)

    @pl.when(j == 0)
    def _():
        seg = seg_ref[i]
        xf = x_ref[...]
        ms = jnp.mean(xf * xf, axis=-1, keepdims=True)
        y = xf * lax.rsqrt(ms + RMS_EPS) * g_ref[...]
        shift = mod_ref[pl.ds(seg, 1), 0:D]
        scale = mod_ref[pl.ds(seg, 1), D:2 * D]
        hx_scr[...] = (y * (1.0 + scale) + shift).astype(bf16)

    tn = w_ref.shape[1]
    pw = min(MXU_COLS, tn)

    def pieces(epilogue):
        for c in range(tn // pw):
            cs = slice(c * pw, (c + 1) * pw)
            epilogue(cs, jnp.dot(hx_scr[...], w_ref[:, cs], preferred_element_type=f32))

    @pl.when((j < 3) | (j == 4))
    def _():
        def epi(cs, acc):
            p_ref[:, cs] = acc.astype(p_ref.dtype)
        pieces(epi)

    @pl.when((j == 3) | (j == 7))
    def _():
        def epi(cs, acc):
            p_ref[:, cs] = (acc * _sigmoid_t(acc)).astype(p_ref.dtype)
        pieces(epi)

    @pl.when((j == 5) | (j == 6))
    def _():
        def epi(cs, acc):
            lb = lb_ref[pl.ds(j - 5, 1), cs]
            f = lb + (1.0 - lb) * jax.nn.sigmoid(acc)
            p_ref[:, cs] = (1.0 - f).astype(p_ref.dtype)
            lg_ref[:, cs] = jnp.log(f)
        pieces(epi)

    @pl.when(j >= 8)
    def _():
        def epi(cs, acc):
            p_ref[:, cs] = _sigmoid_t(acc).astype(p_ref.dtype)
        pieces(epi)


COL_HY, COL_Q, COL_V, COL_KF, COL_KB, COL_OG, COL_GATE = 0, 3, 4, 5, 6, 7, 8


def _inproj(xs, tile_seg, mod_l, g1, w_bf, layer, lb_l, tm):
    T, D = xs.shape
    h = D // 2
    return pl.pallas_call(
        functools.partial(_inproj_kernel, D=D),
        out_shape=(
            jax.ShapeDtypeStruct((T, 12 * h), bf16),
            jax.ShapeDtypeStruct((T, 2 * h), f32),
        ),
        grid_spec=pltpu.PrefetchScalarGridSpec(
            num_scalar_prefetch=1,
            grid=(T // tm, 12),
            in_specs=[
                pl.BlockSpec((tm, D), lambda i, j, s: (i, 0)),
                pl.BlockSpec((8, 6 * D), lambda i, j, s: (0, 0)),
                pl.BlockSpec((1, D), lambda i, j, s: (0, 0)),
                pl.BlockSpec((None, D, h), lambda i, j, s: (layer, 0, j)),
                pl.BlockSpec((2, h), lambda i, j, s: (0, 0)),
            ],
            out_specs=[
                pl.BlockSpec((tm, h), lambda i, j, s: (i, j)),
                pl.BlockSpec((tm, h), lambda i, j, s: (i, jnp.clip(j - COL_KF, 0, 1))),
            ],
            scratch_shapes=[pltpu.VMEM((tm, D), bf16)],
        ),
        compiler_params=_cp(("arbitrary", "arbitrary")),
        name="inproj",
    )(tile_seg, xs, mod_l, g1, w_bf, lb_l)


def _split3(g):
    g1 = g.astype(bf16)
    r1 = g - g1.astype(f32)
    g2 = r1.astype(bf16)
    g3 = (r1 - g2.astype(f32)).astype(bf16)
    return g1, g2, g3


def _chunk_cumsum(g_ref, rev):
    R = g_ref.shape[0]
    ti = lax.broadcasted_iota(i32, (R, R), 0)
    si = lax.broadcasted_iota(i32, (R, R), 1)
    inside = (ti // SCAN_CHUNK) == (si // SCAN_CHUNK)
    tri = jnp.where(inside, jnp.where((si >= ti) if rev else (si <= ti), 1.0, 0.0), 0.0).astype(bf16)
    return sum(jnp.dot(tri, gi, preferred_element_type=f32) for gi in _split3(g_ref[...] * LOG2_E))


def _hgrn_chunk_front(q_ref, k_ref, b_all, r0, *, rev, H):
    C = SCAN_CHUNK
    rows = slice(r0, r0 + C)
    ti = lax.broadcasted_iota(i32, (C, C), 0)
    si = lax.broadcasted_iota(i32, (C, C), 1)
    causal = (si >= ti) if rev else (si <= ti)
    lvl = jnp.where((ti // 32) != (si // 32), 0,
                    jnp.where((ti // 16) != (si // 16), 1,
                              jnp.where((ti // 8) != (si // 8), 2, 3)))
    off = 1 if rev else 0
    out = []
    for h in range(H):
        hs = slice(h * HEAD_DIM, (h + 1) * HEAD_DIM)
        b = b_all[r0:r0 + C, hs]
        qb = q_ref[rows, hs]
        kb = k_ref[rows, hs]

        def rowb(r, n, b=b):
            return jnp.broadcast_to(b[r:r + 1, :], (n, HEAD_DIM))

        refs = [
            rowb(31 + off, 64),
            jnp.concatenate([rowb(15 + off, 32), rowb(47 + off, 32)], axis=0),
            jnp.concatenate([rowb(16 * i + 7 + off, 16) for i in range(4)], axis=0),
        ]
        p = []
        for m in refs:
            p.append(lax.dot_general(qb * jnp.exp2(b - m).astype(bf16), kb * jnp.exp2(m - b).astype(bf16), _NT,
                                     preferred_element_type=f32))
        md = jnp.concatenate([rowb(8 * i + (7 if rev else 0), 8) for i in range(8)], axis=0)
        p.append(lax.dot_general(qb * jnp.exp2(b - md).astype(bf16),
                                 kb * jnp.exp2(jnp.minimum(md - b, DIAG_EXP2_CAP)).astype(bf16), _NT,
                                 preferred_element_type=f32))
        att = jnp.where(lvl == 0, p[0], jnp.where(lvl == 1, p[1], jnp.where(lvl == 2, p[2], p[3])))
        att = jnp.where(causal, att, 0.0).astype(bf16)
        bend = b[0:1, :] if rev else b[C - 1:C, :]
        out.append((att, qb * jnp.exp2(b).astype(bf16), kb * jnp.exp2(bend - b).astype(bf16), jnp.exp2(bend)))
    return out


def _hgrn_chunk_back(front, v_ref, o_ref, s_scr, r0, d, *, H):
    rows = slice(r0, r0 + SCAN_CHUNK)
    for h in range(H):
        hs = slice(h * HEAD_DIM, (h + 1) * HEAD_DIM)
        att, q_in, k_st, dec = front[h]
        v = v_ref[rows, hs]
        st = s_scr[d, h]
        o_h = jnp.dot(att, v, preferred_element_type=f32)
        o_h = o_h + lax.dot_general(q_in, st.astype(bf16), _NT, preferred_element_type=f32)
        o_ref[rows, hs] = o_h.astype(o_ref.dtype)
        s_scr[d, h] = st * dec + lax.dot_general(v, k_st, _TN, preferred_element_type=f32)


def _hgrn_kernel(qf_ref, vf_ref, kf_ref, gf_ref, qb_ref, vb_ref, kb_ref, gb_ref, of_ref, ob_ref, s_scr, *, H, G):
    @pl.when(pl.program_id(1) == 0)
    def _():
        s_scr[...] = jnp.zeros_like(s_scr)

    bf = _chunk_cumsum(gf_ref, False)
    bb = _chunk_cumsum(gb_ref, True)
    C = SCAN_CHUNK
    fronts = []
    for c in range(G):
        fronts.append((_hgrn_chunk_front(qf_ref, kf_ref, bf, c * C, rev=False, H=H),
                       _hgrn_chunk_front(qb_ref, kb_ref, bb, (G - 1 - c) * C, rev=True, H=H)))
    for c in range(G):
        _hgrn_chunk_back(fronts[c][0], vf_ref, of_ref, s_scr, c * C, 0, H=H)
        _hgrn_chunk_back(fronts[c][1], vb_ref, ob_ref, s_scr, (G - 1 - c) * C, 1, H=H)


def _hgrn_scan(hg, lg, B, L, Lc):
    T = B * (L + Lc)
    h = lg.shape[1] // 2
    H = h // HEAD_DIM
    G = min(4, Lc // SCAN_CHUNK)
    R = G * SCAN_CHUNK
    assert L % R == 0 and Lc % R == 0
    nL, nC = L // R, Lc // R

    def fwd(b, s):
        return jnp.where(s < nC, B * nL + b * nC + s, b * nL + (s - nC))

    def bwd(b, s):
        return jnp.where(s < nC, B * nL + b * nC + (nC - 1 - s), b * nL + (nL - 1 - (s - nC)))

    def spec(rowfn, col):
        return pl.BlockSpec((R, h), lambda b, s: (rowfn(b, s), col))

    return pl.pallas_call(
        functools.partial(_hgrn_kernel, H=H, G=G),
        out_shape=(jax.ShapeDtypeStruct((T, h), bf16), jax.ShapeDtypeStruct((T, h), bf16)),
        grid=(B, nC + nL),
        in_specs=[spec(fwd, COL_Q), spec(fwd, COL_V), spec(fwd, COL_KF), spec(fwd, 0),
                  spec(bwd, COL_Q), spec(bwd, COL_V), spec(bwd, COL_KB), spec(bwd, 1)],
        out_specs=[spec(fwd, 0), spec(bwd, 0)],
        scratch_shapes=[pltpu.VMEM((2, H, HEAD_DIM, HEAD_DIM), f32)],
        compiler_params=_cp(("arbitrary", "arbitrary")),
        name="hgrn_scan",
    )(hg, hg, hg, lg, hg, hg, hg, lg)


def _shortconv_kernel(p_ref, pv_ref, nx_ref, w_ref, b_ref, ul_ref, uc_ref, *, R, BL, L, Lc):
    i = pl.program_id(1)
    r0 = i * R
    lat = r0 < BL
    first = jnp.where(lat, (r0 % L) == 0, ((r0 - BL) % Lc) == 0)
    last = jnp.where(lat, ((r0 + R) % L) == 0, ((r0 + R - BL) % Lc) == 0)
    ri = lax.broadcasted_iota(i32, (R, LANES), 0)

    def conv_cols(o_ref):
        for c in range(p_ref.shape[1] // LANES):
            cs = slice(c * LANES, (c + 1) * LANES)
            p = p_ref[:, cs].astype(f32)
            prev_row = jnp.where(first, 0.0, pv_ref[15:16, cs].astype(f32))
            next_row = jnp.where(last, 0.0, nx_ref[0:1, cs].astype(f32))
            pm = jnp.where(ri == 0, prev_row, pltpu.roll(p, 1, 0))
            pp = jnp.where(ri == R - 1, next_row, pltpu.roll(p, R - 1, 0))
            o_ref[:, cs] = pm * w_ref[0:1, cs] + p * w_ref[1:2, cs] + pp * w_ref[2:3, cs] + b_ref[:, cs]

    @pl.when(lat)
    def _():
        conv_cols(ul_ref)

    @pl.when(jnp.logical_not(lat))
    def _():
        conv_cols(uc_ref)


def _shortconv(p_hy, conv_w, conv_b, B, L, Lc):
    h = p_hy.shape[1] // 12
    W3 = 3 * h
    BL, BLc = B * L, B * Lc
    T = BL + BLc
    R = min(256, Lc)
    nlat = BL // R
    return pl.pallas_call(
        functools.partial(_shortconv_kernel, R=R, BL=BL, L=L, Lc=Lc),
        out_shape=(jax.ShapeDtypeStruct((BL, W3), f32), jax.ShapeDtypeStruct((BLc, W3), f32)),
        grid=(3, T // R),
        in_specs=[
            pl.BlockSpec((R, h), lambda j, i: (i, j)),
            pl.BlockSpec((16, h), lambda j, i: (jnp.maximum(i * (R // 16) - 1, 0), j)),
            pl.BlockSpec((16, h), lambda j, i: (jnp.minimum((i + 1) * (R // 16), T // 16 - 1), j)),
            pl.BlockSpec((3, h), lambda j, i: (0, j)),
            pl.BlockSpec((1, h), lambda j, i: (0, j)),
        ],
        out_specs=[
            pl.BlockSpec((R, h), lambda j, i: (jnp.minimum(i, nlat - 1), j)),
            pl.BlockSpec((R, h), lambda j, i: (jnp.maximum(i - nlat, 0), j)),
        ],
        compiler_params=_cp(("arbitrary", "arbitrary")),
        name="shortconv",
    )(p_hy, p_hy, p_hy, conv_w, conv_b.reshape(1, W3))


def _filter_taps(z, fw1, fb1, fw2, fb2, fw3, fb3, wo_first, wo_second, freq, deltas, nfirst):
    h = jnp.sin(freq[0:1, :] * (jnp.dot(z, fw1, preferred_element_type=f32, precision=HIGHEST) + fb1))
    h = jnp.sin(freq[1:2, :] * (jnp.dot(h, fw2, preferred_element_type=f32, precision=HIGHEST) + fb2))
    h = jnp.sin(freq[2:3, :] * (jnp.dot(h, fw3, preferred_element_type=f32, precision=HIGHEST) + fb3))
    hh = h.astype(bf16)
    hl = (h - hh.astype(f32)).astype(bf16)
    h3 = jnp.concatenate([hh, hl, hh], axis=1)
    a = jnp.dot(h3[:nfirst], wo_first, preferred_element_type=f32)
    b = jnp.dot(h3[nfirst:], wo_second, preferred_element_type=f32)
    taps = jnp.concatenate([a, b], axis=0)
    return taps * jnp.exp(-z[:, 0:1] * deltas) * z[:, EMB_PAD - 1:EMB_PAD]


def _ctx_filter_kernel(z_ref, fw1, fb1, fw2, fb2, fw3, fb3, wo0, wo1, freq, dl, o_ref, *, Lc):
    taps = _filter_taps(z_ref[...], fw1[...], fb1[...], fw2[...], fb2[...], fw3[...], fb3[...],
                        wo1[...], wo0[...], freq[...], dl[...], Lc)
    o_ref[...] = taps / jnp.sum(jnp.abs(taps), axis=0, keepdims=True)


def _ctx_filter(ztab, fl, Lc, h):
    fw1, fb1, fw2, fb2, fw3, fb3, fwout, freq, deltas = fl
    cb = min(512, 2 * h)
    nb = (2 * h) // cb
    full = lambda a: pl.BlockSpec(a.shape, lambda j: (0,) * a.ndim)
    return pl.pallas_call(
        functools.partial(_ctx_filter_kernel, Lc=Lc),
        out_shape=jax.ShapeDtypeStruct((2 * Lc, 2 * h), f32),
        grid=(nb,),
        in_specs=[full(ztab), full(fw1), full(fb1), full(fw2), full(fb2), full(fw3), full(fb3),
                  pl.BlockSpec((fwout.shape[0], cb), lambda j: (0, j)),
                  pl.BlockSpec((fwout.shape[0], cb), lambda j: (0, nb + j)),
                  full(freq), pl.BlockSpec((1, cb), lambda j: (0, j))],
        out_specs=pl.BlockSpec((2 * Lc, cb), lambda j: (0, j)),
        compiler_params=_cp(("arbitrary",)),
        name="ctx_filter",
    )(ztab, fw1, fb1, fw2, fb2, fw3, fb3, fwout, fwout, freq, deltas)


def _ctx_conv_kernel(v_ref, x1_ref, x2_ref, k1_ref, k2_ref, b1_ref, b2_ref, o_ref, u_scr, *, Lc):
    def conv(kk_ref):
        def body(s, acc):
            return acc + kk_ref[pl.ds(Lc - s, Lc), :] * u_scr[pl.ds(s, 1), :]
        return lax.fori_loop(0, Lc, body, jnp.zeros((Lc, 128), f32))

    v = v_ref[...]
    u_scr[...] = v
    z = x1_ref[...] * (conv(k1_ref) + b1_ref[...] * v)
    u_scr[...] = z
    o_ref[...] = x2_ref[...] * (conv(k2_ref) + b2_ref[...] * z)


def _ctx_conv(u_ctx, kk, bias, B, Lc):
    h = u_ctx.shape[1] // 3
    nb = h // 128
    return pl.pallas_call(
        functools.partial(_ctx_conv_kernel, Lc=Lc),
        out_shape=jax.ShapeDtypeStruct((B * Lc, h), f32),
        grid=(B, nb),
        in_specs=[
            pl.BlockSpec((Lc, 128), lambda b, j: (b, j)),
            pl.BlockSpec((Lc, 128), lambda b, j: (b, nb + j)),
            pl.BlockSpec((Lc, 128), lambda b, j: (b, 2 * nb + j)),
            pl.BlockSpec((2 * Lc, 128), lambda b, j: (0, j)),
            pl.BlockSpec((2 * Lc, 128), lambda b, j: (0, nb + j)),
            pl.BlockSpec((1, 128), lambda b, j: (0, j)),
            pl.BlockSpec((1, 128), lambda b, j: (0, nb + j)),
        ],
        out_specs=pl.BlockSpec((Lc, 128), lambda b, j: (b, j)),
        scratch_shapes=[pltpu.VMEM((Lc, 128), f32)],
        compiler_params=_cp(("arbitrary", "arbitrary")),
        name="ctx_conv",
    )(u_ctx, u_ctx, u_ctx, kk, kk, bias.reshape(1, 2 * h), bias.reshape(1, 2 * h))


def _dft_constants(L):
    N2 = FFT_N2
    N1h = L // N2
    N1 = 2 * N1h
    N = N1 * N2
    k1 = jnp.arange(N1, dtype=i32)
    n1 = jnp.arange(N1, dtype=i32)
    n2 = jnp.arange(N2, dtype=i32)
    ph = (k1[None, :, None] * (n1[None, None, :] * N2 + n2[:, None, None])) % N
    ang = ph.astype(f32) * (2.0 * math.pi / N)
    gr, gi = jnp.cos(ang), -jnp.sin(ang)
    grh, gih = gr[:, :, :N1h], gi[:, :, :N1h]
    g_fwd = jnp.concatenate([jnp.concatenate([grh, -gih], axis=2),
                             jnp.concatenate([gih, grh], axis=2)], axis=1).astype(bf16)
    g_real = jnp.concatenate([gr, gi], axis=1).astype(bf16)
    mr = jnp.swapaxes(grh, 1, 2)
    mi = -jnp.swapaxes(gih, 1, 2)
    g_inv = jnp.concatenate([jnp.concatenate([mr, -mi], axis=2),
                             jnp.concatenate([mi, mr], axis=2)], axis=1).astype(bf16)
    kk = jnp.arange(N2, dtype=i32)
    a2 = ((kk[:, None] * kk[None, :]) % N2).astype(f32) * (2.0 * math.pi / N2)
    fr, fi = jnp.cos(a2), -jnp.sin(a2)
    fb_fwd = jnp.concatenate([jnp.concatenate([fr, -fi], axis=1),
                              jnp.concatenate([fi, fr], axis=1)], axis=0).astype(bf16)
    fb_inv = jnp.concatenate([jnp.concatenate([fr, fi], axis=1),
                              jnp.concatenate([-fi, fr], axis=1)], axis=0).astype(bf16)
    return dict(N1h=N1h, N1=N1, N2=N2, N=N, g_fwd=g_fwd, g_real=g_real, g_inv=g_inv, fb_fwd=fb_fwd, fb_inv=fb_inv)


def _strided_rows(refs, start, count, stride):
    parts = []
    for ref in refs:
        assert ref.shape[-1] == LANES
        flat = ref.reshape(math.prod(ref.shape[:-1]), LANES)
        parts.append(flat[pl.ds(start, count, stride=stride), :])
    return parts[0] if len(parts) == 1 else jnp.concatenate(parts, axis=1)


u32 = jnp.uint32


def _pack_pair(y):
    n = y.shape[0] // 2
    hi = lax.bitcast_convert_type(y[:n].astype(bf16).astype(f32), u32)
    lo = lax.bitcast_convert_type(y[n:].astype(bf16).astype(f32), u32)
    return hi | (lo >> 16)


def _unpack_pair(w):
    re = lax.bitcast_convert_type(w & u32(0xFFFF0000), f32)
    im = lax.bitcast_convert_type(w << 16, f32)
    return jnp.concatenate([re, im], axis=0).astype(bf16)


def _pick(refs, j):
    n = refs[0].shape[0]
    if refs[0].shape[-1] == LANES:
        return _unpack_pair(_strided_rows(refs, j, n, 8))
    (ref,) = refs
    return _unpack_pair(ref[:, j, :])


def _filt_a_kernel(z_ref, fw1, fb1, fw2, fb2, fw3, fb3, wo0, wo1, freq, dl, g_ref, y_ref, l1_ref, *, N1h):
    @pl.when(pl.program_id(0) == 0)
    def _():
        l1_ref[...] = jnp.zeros_like(l1_ref)

    for q in range(z_ref.shape[0]):
        taps = _filter_taps(z_ref[q], fw1[...], fb1[...], fw2[...], fb2[...], fw3[...], fb3[...],
                            wo0[...], wo1[...], freq[...], dl[...], N1h)
        l1_ref[...] += jnp.sum(jnp.abs(taps), axis=0, keepdims=True)
        y_ref[q] = _pack_pair(jnp.dot(g_ref[q], taps.astype(bf16), preferred_element_type=f32))


def _filt_b_kernel(ya_ref, yb_ref, fb_ref, l1_ref, k_ref, *, N):
    scale = 1.0 / (l1_ref[...] * float(N))
    for j in range(8):
        kf = jnp.dot(fb_ref[...], _pick((ya_ref, yb_ref), j), preferred_element_type=f32) * scale
        k_ref[j] = kf.astype(k_ref.dtype)


def _latent_filter(ztab, fl, dc, h):
    fw1, fb1, fw2, fb2, fw3, fb3, fwout, freq, deltas = fl
    N1h, N1, N2, N = dc["N1h"], dc["N1"], dc["N2"], dc["N"]
    full = lambda a: pl.BlockSpec(a.shape, lambda n: (0,) * a.ndim)
    yk, l1 = pl.pallas_call(
        functools.partial(_filt_a_kernel, N1h=N1h),
        out_shape=(jax.ShapeDtypeStruct((N2, N1, 2 * h), u32), jax.ShapeDtypeStruct((1, 2 * h), f32)),
        grid=(N2 // FILT_Q,),
        in_specs=[pl.BlockSpec((FILT_Q, N1, EMB_PAD), lambda n: (n, 0, 0)),
                  full(fw1), full(fb1), full(fw2), full(fb2), full(fw3), full(fb3),
                  pl.BlockSpec((fwout.shape[0], 2 * h), lambda n: (0, 0)),
                  pl.BlockSpec((fwout.shape[0], 2 * h), lambda n: (0, 1)),
                  full(freq), full(deltas),
                  pl.BlockSpec((FILT_Q, 2 * N1, N1), lambda n: (n, 0, 0))],
        out_specs=[pl.BlockSpec((FILT_Q, N1, 2 * h), lambda n: (n, 0, 0)),
                   pl.BlockSpec((1, 2 * h), lambda n: (0, 0))],
        compiler_params=_cp(("arbitrary",)),
        name="filter_stage_a",
    )(ztab, fw1, fb1, fw2, fb2, fw3, fb3, fwout, fwout, freq, deltas, dc["g_real"])
    cb = 2 * LANES
    return pl.pallas_call(
        functools.partial(_filt_b_kernel, N=N),
        out_shape=jax.ShapeDtypeStruct((N1, 2 * N2, 2 * h), bf16),
        grid=(N1 // 8, (2 * h) // cb),
        in_specs=[pl.BlockSpec((N2, 8, LANES), lambda i, c: (0, i, 2 * c)),
                  pl.BlockSpec((N2, 8, LANES), lambda i, c: (0, i, 2 * c + 1)),
                  pl.BlockSpec((2 * N2, 2 * N2), lambda i, c: (0, 0)),
                  pl.BlockSpec((1, cb), lambda i, c: (0, c))],
        out_specs=pl.BlockSpec((8, 2 * N2, cb), lambda i, c: (i, 0, c)),
        compiler_params=_cp(("arbitrary", "arbitrary")),
        name="filter_stage_b",
    )(yk, yk, dc["fb_fwd"], l1)


def _stack_batches(ref, j):
    if ref.shape[-1] == LANES:
        return _strided_rows((ref,), j, ref.shape[0] * ref.shape[1], 8)
    return jnp.concatenate([ref[0, :, j, :], ref[1, :, j, :]], axis=0)


def _conv_a_kernel(u_ref, g_ref, y_ref):
    for j in range(8):
        y_ref[j] = _pack_pair(jnp.dot(g_ref[j], _stack_batches(u_ref, j).astype(bf16), preferred_element_type=f32))


def _conv_b_kernel(ya_ref, yb_ref, fbf_ref, fbi_ref, k_ref, w_ref, *, N2):
    for j in range(8):
        z = jnp.dot(fbf_ref[...], _pick((ya_ref, yb_ref), j), preferred_element_type=f32)
        zr, zi = z[:N2], z[N2:]
        kr, ki = k_ref[j, :N2, :].astype(f32), k_ref[j, N2:, :].astype(f32)
        p = jnp.concatenate([zr * kr - zi * ki, zr * ki + zi * kr], axis=0).astype(bf16)
        w_ref[j] = _pack_pair(jnp.dot(fbi_ref[...], p, preferred_element_type=f32))


def _conv_mid_kernel(w_ref, gi_ref, g_ref, v_ref, x1_ref, b_ref, z_ref, y_ref):
    for j in range(8):
        y = jnp.dot(gi_ref[j], _pick((w_ref,), j), preferred_element_type=f32)
        vv = _stack_batches(v_ref, j)
        z = _stack_batches(x1_ref, j) * (y + b_ref[...] * vv)
        z_ref[j] = z
        y_ref[j] = _pack_pair(jnp.dot(g_ref[j], z.astype(bf16), preferred_element_type=f32))


def _store_strided_rows(ref, start, stride, val):
    assert ref.shape[-1] == LANES
    flat = ref.reshape(math.prod(ref.shape[:-1]), LANES)
    flat[pl.ds(start, val.shape[0], stride=stride), :] = val


def _conv_out_kernel(w_ref, gi_ref, z_ref, x2_ref, b_ref, o_ref):
    for j in range(8):
        y = jnp.dot(gi_ref[j], _pick((w_ref,), j), preferred_element_type=f32)
        o = _stack_batches(x2_ref, j) * (y + b_ref[...] * z_ref[j])
        if o_ref.shape[-1] == LANES:
            _store_strided_rows(o_ref, j, 8, o)
        else:
            nh = o_ref.shape[1]
            o_ref[0, :, j, :] = o[:nh]
            o_ref[1, :, j, :] = o[nh:]


def _latent_hyena(u_lat, kf, bias, dc, B, L):
    assert B == 2, "the long convolution packs exactly two batch rows into one complex sequence"
    h = u_lat.shape[1] // 3
    N1h, N1, N2 = dc["N1h"], dc["N1"], dc["N2"]
    u4 = u_lat.reshape(B, N1h, N2, 3 * h)
    cb = min(512, h)
    nb = h // cb
    cb2 = 2 * LANES
    nb2 = h // cb2
    bias2 = bias.reshape(1, 2 * h)
    ublk = lambda col: pl.BlockSpec((B, N1h, 8, cb), lambda i, c: (0, 0, i, col * nb + c))

    ya = pl.pallas_call(
        _conv_a_kernel,
        out_shape=jax.ShapeDtypeStruct((N2, N1, h), u32),
        grid=(N2 // 8, nb),
        in_specs=[ublk(0), pl.BlockSpec((8, 2 * N1, 2 * N1h), lambda i, c: (i, 0, 0))],
        out_specs=pl.BlockSpec((8, N1, cb), lambda i, c: (i, 0, c)),
        compiler_params=_cp(("arbitrary", "arbitrary")),
        name="conv_stage_a",
    )(u4, dc["g_fwd"])

    def stage_b(y, order):
        return pl.pallas_call(
            functools.partial(_conv_b_kernel, N2=N2),
            out_shape=jax.ShapeDtypeStruct((N1, N2, h), u32),
            grid=(N1 // 8, nb2),
            in_specs=[pl.BlockSpec((N2, 8, LANES), lambda i, c: (0, i, 2 * c)),
                      pl.BlockSpec((N2, 8, LANES), lambda i, c: (0, i, 2 * c + 1)),
                      pl.BlockSpec((2 * N2, 2 * N2), lambda i, c: (0, 0)),
                      pl.BlockSpec((2 * N2, 2 * N2), lambda i, c: (0, 0)),
                      pl.BlockSpec((8, 2 * N2, cb2), lambda i, c: (i, 0, order * nb2 + c))],
            out_specs=pl.BlockSpec((8, N2, cb2), lambda i, c: (i, 0, c)),
            compiler_params=_cp(("arbitrary", "arbitrary")),
            name="conv_stage_b",
        )(y, y, dc["fb_fwd"], dc["fb_inv"], kf)

    w1 = stage_b(ya, 0)
    zp, ya2 = pl.pallas_call(
        _conv_mid_kernel,
        out_shape=(jax.ShapeDtypeStruct((N2, 2 * N1h, h), f32), jax.ShapeDtypeStruct((N2, N1, h), u32)),
        grid=(N2 // 8, nb),
        in_specs=[pl.BlockSpec((N1, 8, cb), lambda i, c: (0, i, c)),
                  pl.BlockSpec((8, 2 * N1h, 2 * N1), lambda i, c: (i, 0, 0)),
                  pl.BlockSpec((8, 2 * N1, 2 * N1h), lambda i, c: (i, 0, 0)),
                  ublk(0), ublk(1),
                  pl.BlockSpec((1, cb), lambda i, c: (0, c))],
        out_specs=[pl.BlockSpec((8, 2 * N1h, cb), lambda i, c: (i, 0, c)),
                   pl.BlockSpec((8, N1, cb), lambda i, c: (i, 0, c))],
        compiler_params=_cp(("arbitrary", "arbitrary")),
        name="conv_stage_mid",
    )(w1, dc["g_inv"], dc["g_fwd"], u4, u4, bias2)
    w2 = stage_b(ya2, 1)
    y = pl.pallas_call(
        _conv_out_kernel,
        out_shape=jax.ShapeDtypeStruct((B, N1h, N2, h), f32),
        grid=(N2 // 8, nb),
        in_specs=[pl.BlockSpec((N1, 8, cb), lambda i, c: (0, i, c)),
                  pl.BlockSpec((8, 2 * N1h, 2 * N1), lambda i, c: (i, 0, 0)),
                  pl.BlockSpec((8, 2 * N1h, cb), lambda i, c: (i, 0, c)),
                  ublk(2),
                  pl.BlockSpec((1, cb), lambda i, c: (0, nb + c))],
        out_specs=pl.BlockSpec((B, N1h, 8, cb), lambda i, c: (0, 0, i, c)),
        compiler_params=_cp(("arbitrary", "arbitrary")),
        name="conv_stage_out",
    )(w2, dc["g_inv"], zp, u4, bias2)
    return y.reshape(B * L, h)


def _merge_kernel(of_ref, ob_ref, og_ref, ng_ref, yl_ref, yc_ref, gt_hy_ref, gt_hg_ref, phy_ref, phg_ref, u_ref,
                  *, H, nlat):
    i = pl.program_id(0)
    o = of_ref[...].astype(f32) + ob_ref[...].astype(f32)
    og = og_ref[...].astype(f32)
    parts = []
    for h in range(H):
        hs = slice(h * HEAD_DIM, (h + 1) * HEAD_DIM)
        oh = o[:, hs]
        r = lax.rsqrt(jnp.mean(oh * oh, axis=-1, keepdims=True) + RMS_EPS)
        parts.append((oh * r * ng_ref[...] * og[:, hs]).astype(bf16))
    y_hg = jnp.concatenate(parts, axis=1)
    y_hy = jnp.where(i < nlat, yl_ref[...], yc_ref[...]).astype(bf16)
    a = jnp.dot(y_hy, phy_ref[...], preferred_element_type=f32)
    b = jnp.dot(y_hg, phg_ref[...], preferred_element_type=f32)
    u_ref[...] = (gt_hy_ref[...].astype(f32) * a + gt_hg_ref[...].astype(f32) * b).astype(u_ref.dtype)


def _merge(o_f, o_b, hg, ng, y_lat, y_ctx, gates, phy, phg, layer, tm, ntile):
    h = o_f.shape[1]
    D = phy.shape[2]
    H = h // HEAD_DIM
    nlat = y_lat.shape[0] // tm
    nctx = y_ctx.shape[0] // tm
    return pl.pallas_call(
        functools.partial(_merge_kernel, H=H, nlat=nlat),
        out_shape=jax.ShapeDtypeStruct((ntile * tm, D), bf16),
        grid=(ntile,),
        in_specs=[
            pl.BlockSpec((tm, h), lambda i: (i, 0)),
            pl.BlockSpec((tm, h), lambda i: (i, 0)),
            pl.BlockSpec((tm, h), lambda i: (i, COL_OG)),
            pl.BlockSpec((1, HEAD_DIM), lambda i: (0, 0)),
            pl.BlockSpec((tm, h), lambda i: (jnp.minimum(i, nlat - 1), 0)),
            pl.BlockSpec((tm, h), lambda i: (jnp.clip(i - nlat, 0, nctx - 1), 0)),
            pl.BlockSpec((tm, D), lambda i: (i, COL_GATE // 2)),
            pl.BlockSpec((tm, D), lambda i: (i, COL_GATE // 2 + 1)),
            pl.BlockSpec((None, h, D), lambda i: (layer, 0, 0)),
            pl.BlockSpec((None, h, D), lambda i: (layer, 0, 0)),
        ],
        out_specs=pl.BlockSpec((tm, D), lambda i: (i, 0)),
        compiler_params=_cp(("arbitrary",)),
        name="merge",
    )(o_f, o_b, hg, ng, y_lat, y_ctx, gates, gates, phy, phg)


def _norm_mod(x, g, shift, scale):
    ms = jnp.mean(x * x, axis=-1, keepdims=True)
    return (x * lax.rsqrt(ms + RMS_EPS) * g) * (1.0 + scale) + shift


def _outproj_kernel(seg_ref, u_ref, w_ref, x_ref, mod_ref, g2_ref, rh_ref, rl_ref, xo_ref, lo_ref, *, D):
    seg = seg_ref[pl.program_id(0)]
    mix = jnp.dot(u_ref[...], w_ref[...], preferred_element_type=f32)
    gate = mod_ref[pl.ds(seg, 1), 2 * D:3 * D]
    xn = x_ref[...] + gate * mix
    xo_ref[...] = xn
    t2 = _norm_mod(xn, g2_ref[...], mod_ref[pl.ds(seg, 1), 3 * D:4 * D], mod_ref[pl.ds(seg, 1), 4 * D:5 * D])
    th = t2.astype(bf16)
    tl = (t2 - th.astype(f32)).astype(bf16)
    lo_ref[...] = (lax.dot_general(rh_ref[...], th, _NT, preferred_element_type=f32)
                   + lax.dot_general(rh_ref[...], tl, _NT, preferred_element_type=f32)
                   + lax.dot_general(rl_ref[...], th, _NT, preferred_element_type=f32))


def _outproj(u, w_out, layer, xs, tile_seg, mod_l, g2, rh, rl, tm, ntile):
    D = w_out.shape[1]
    E = rh.shape[0]
    return pl.pallas_call(
        functools.partial(_outproj_kernel, D=D),
        out_shape=(jax.ShapeDtypeStruct((ntile * tm, D), f32), jax.ShapeDtypeStruct((E, ntile * tm), f32)),
        grid_spec=pltpu.PrefetchScalarGridSpec(
            num_scalar_prefetch=1,
            grid=(ntile,),
            in_specs=[
                pl.BlockSpec((tm, D), lambda i, s: (i, 0)),
                pl.BlockSpec((None, D, D), lambda i, s: (layer, 0, 0)),
                pl.BlockSpec((tm, D), lambda i, s: (i, 0)),
                pl.BlockSpec((8, 6 * D), lambda i, s: (0, 0)),
                pl.BlockSpec((1, D), lambda i, s: (0, 0)),
                pl.BlockSpec((E, D), lambda i, s: (0, 0)),
                pl.BlockSpec((E, D), lambda i, s: (0, 0)),
            ],
            out_specs=[pl.BlockSpec((tm, D), lambda i, s: (i, 0)),
                       pl.BlockSpec((E, tm), lambda i, s: (0, i))],
        ),
        compiler_params=_cp(("arbitrary",)),
        name="outproj",
    )(tile_seg, u, w_out, xs, mod_l, g2, rh, rl)


_PAIRS = ((0, 1), (0, 2), (0, 3), (1, 2), (1, 3), (2, 3))


def _route_kernel(lo_ref, rb_ref, o_ref, *, E):
    per = E // N_GROUPS
    assert per == 4
    lo = lo_ref[...]
    sc = jax.nn.sigmoid(lo)
    sel = sc + rb_ref[...]
    srow = [sel[e:e + 1, :] for e in range(E)]
    crow = [sc[e:e + 1, :] for e in range(E)]
    gs = []
    for g in range(N_GROUPS):
        x = srow[per * g:per * g + per]
        m = x[0] + x[1]
        for (a, b) in _PAIRS[1:]:
            m = jnp.maximum(m, x[a] + x[b])
        gs.append(m)
    gbest = jnp.zeros_like(gs[0]).astype(i32)
    best = gs[0]
    for g in range(1, N_GROUPS):
        better = gs[g] > best
        gbest = jnp.where(better, g, gbest)
        best = jnp.where(better, gs[g], best)

    def pick(rows, i):
        out = rows[i]
        for g in range(1, N_GROUPS):
            out = jnp.where(gbest == g, rows[per * g + i], out)
        return out

    x = [pick(srow, i) for i in range(per)]
    s = [pick(crow, i) for i in range(per)]
    chosen = []
    for i in range(per):
        cnt = jnp.zeros_like(gbest)
        for j in range(per):
            if j == i:
                continue
            beats = (x[j] >= x[i]) if j < i else (x[j] > x[i])
            cnt = cnt + jnp.where(beats, 1, 0)
        chosen.append(cnt < 2)
    pair = jnp.zeros_like(gbest)
    wa = jnp.zeros_like(best)
    wb = jnp.zeros_like(best)
    for p, (a, b) in enumerate(_PAIRS):
        hit = jnp.where(chosen[a], jnp.where(chosen[b], 1, 0), 0) == 1
        pair = jnp.where(hit, p, pair)
        wa = jnp.where(hit, s[a], wa)
        wb = jnp.where(hit, s[b], wb)
    tot = wa + wb
    cls = (gbest * len(_PAIRS) + pair).astype(f32)
    o_ref[...] = jnp.concatenate([cls, wa / tot, wb / tot, jnp.zeros((5, cls.shape[1]), f32)], axis=0)


def _route(logits_t, router_b, tl):
    E, T = logits_t.shape
    return pl.pallas_call(
        functools.partial(_route_kernel, E=E),
        out_shape=jax.ShapeDtypeStruct((8, T), f32),
        grid=(T // tl,),
        in_specs=[pl.BlockSpec((E, tl), lambda i: (0, i)), pl.BlockSpec((E, 1), lambda i: (0, 0))],
        out_specs=pl.BlockSpec((8, tl), lambda i: (0, i)),
        compiler_params=_cp(("arbitrary",)),
        name="route",
    )(logits_t, router_b.reshape(E, 1))


def _moe_kernel(be_ref, nb_ref, x_ref, rw_ref, mod_ref, g2_ref, w1_ref, w3_ref, w2_ref, o_ref, h_scr, acc_scr, *, D, nseg):
    i = pl.program_id(0)
    k = pl.program_id(1)
    f = pl.program_id(2)
    nf = pl.num_programs(2)
    live = i < nb_ref[0]

    def seg_rows(col0):
        seg = rw_ref[:, 2:3]
        out = mod_ref[0:1, col0:col0 + D]
        for r in range(1, nseg):
            out = jnp.where(seg == float(r), mod_ref[r:r + 1, col0:col0 + D], out)
        return out

    @pl.when(live & (k == 0) & (f == 0))
    def _():
        h_scr[...] = _norm_mod(x_ref[...], g2_ref[...], seg_rows(3 * D), seg_rows(4 * D)).astype(bf16)
        acc_scr[...] = jnp.zeros_like(acc_scr)

    @pl.when(live)
    def _():
        hx = h_scr[...]
        a = jnp.dot(hx, w1_ref[...], preferred_element_type=f32)
        b = jnp.dot(hx, w3_ref[...], preferred_element_type=f32)
        wsel = jnp.where(k == 0, rw_ref[:, 0:1], rw_ref[:, 1:2])
        act = (_silu(a) * b * wsel).astype(bf16)
        acc_scr[...] += jnp.dot(act, w2_ref[...], preferred_element_type=f32)

    @pl.when(live & (k == 1) & (f == nf - 1))
    def _():
        o_ref[...] = x_ref[...] + seg_rows(5 * D) * acc_scr[...]

    @pl.when(jnp.logical_not(live) & (k == 1) & (f == nf - 1))
    def _():
        o_ref[...] = x_ref[...]


def _moe(xg, rw, blk_e, nblk_used, mod_l, g2, w1, w3, w2, layer, nseg):
    P, D = xg.shape
    F = w1.shape[3]
    tf = min(512, F)
    nblk = P // MOE_ROWS
    nf = F // tf

    def wsel(i, k, f, be, nb):
        live = i < nb[0]
        e = be[2 * jnp.minimum(i, nb[0] - 1) + jnp.where(live, k, 1)]
        return e, jnp.where(live, f, nf - 1)

    def w13(i, k, f, be, nb):
        e, ff = wsel(i, k, f, be, nb)
        return (layer, e, 0, ff)

    def w2m(i, k, f, be, nb):
        e, ff = wsel(i, k, f, be, nb)
        return (layer, e, ff, 0)

    return pl.pallas_call(
        functools.partial(_moe_kernel, D=D, nseg=nseg),
        out_shape=jax.ShapeDtypeStruct((P, D), f32),
        grid_spec=pltpu.PrefetchScalarGridSpec(
            num_scalar_prefetch=2,
            grid=(nblk, 2, nf),
            in_specs=[
                pl.BlockSpec((MOE_ROWS, D), lambda i, k, f, be, nb: (i, 0)),
                pl.BlockSpec((MOE_ROWS, 128), lambda i, k, f, be, nb: (i, 0)),
                pl.BlockSpec((8, 6 * D), lambda i, k, f, be, nb: (0, 0)),
                pl.BlockSpec((1, D), lambda i, k, f, be, nb: (0, 0)),
                pl.BlockSpec((None, None, D, tf), w13),
                pl.BlockSpec((None, None, D, tf), w13),
                pl.BlockSpec((None, None, tf, D), w2m),
            ],
            out_specs=pl.BlockSpec((MOE_ROWS, D), lambda i, k, f, be, nb: (i, 0)),
            scratch_shapes=[pltpu.VMEM((MOE_ROWS, D), bf16), pltpu.VMEM((MOE_ROWS, D), f32)],
        ),
        compiler_params=_cp(("arbitrary", "arbitrary", "arbitrary")),
        name="moe_experts",
    )(blk_e, nblk_used, xg, rw, mod_l, g2, w1, w3, w2)


def _final_norm_kernel(x_ref, g_ref, o_ref):
    x = x_ref[...]
    ms = jnp.mean(x * x, axis=-1, keepdims=True)
    o_ref[...] = x * lax.rsqrt(ms + RMS_EPS) * g_ref[...]


def _final_norm(x, g, tm):
    n, D = x.shape
    return pl.pallas_call(
        _final_norm_kernel,
        out_shape=jax.ShapeDtypeStruct((n, D), f32),
        grid=(n // tm,),
        in_specs=[pl.BlockSpec((tm, D), lambda i: (i, 0)), pl.BlockSpec((1, D), lambda i: (0, 0))],
        out_specs=pl.BlockSpec((tm, D), lambda i: (i, 0)),
        compiler_params=_cp(("arbitrary",)),
        name="final_norm",
    )(x, g)


def _feature_rows(pos, valid, Lx, bands):
    t = jnp.linspace(0.0, 1.0, Lx, dtype=f32)[jnp.clip(pos, 0, Lx - 1)][..., None]
    w = ((2.0 * math.pi / Lx) * jnp.clip(pos, 0, Lx - 1).astype(f32))[..., None]
    fb = jnp.linspace(1e-4, bands - 1, bands, dtype=f32)
    z = jnp.concatenate([t, jnp.cos(fb * w), -jnp.sin(fb * w)], axis=-1)
    z = jnp.pad(z, [(0, 0)] * (z.ndim - 1) + [(0, EMB_PAD - 1 - z.shape[-1])])
    return jnp.concatenate([z, valid.astype(f32)[..., None]], axis=-1)


def _latent_ztab(L, bands, dc):
    N1, N2 = dc["N1"], dc["N2"]
    tau = jnp.arange(N1, dtype=i32)[None, :] * N2 + jnp.arange(N2, dtype=i32)[:, None]
    pos = jnp.where(tau < L, tau, 2 * L - tau)
    return _feature_rows(pos, tau != L, L, bands)


def _ctx_ztab(Lc, bands):
    r = jnp.arange(2 * Lc, dtype=i32)
    return _feature_rows(jnp.abs(r - Lc), r != 0, Lc, bands)


def kernel(x, c, ctx, c_ctx, ada_w, ada_b, norm1_g, norm2_g, final_g, w_in, hy_conv_w, hy_conv_b, hy_fw1, hy_fb1,
           hy_fw2, hy_fb2, hy_fw3, hy_fb3, hy_fwout, hy_freq, hy_bias, hg_lb_raw, hg_norm_g, p_hy, p_hg, w_out,
           router_w, router_b, moe_w1, moe_w3, moe_w2):
    B, L, D = x.shape
    Lc = ctx.shape[1]
    depth = ada_w.shape[0]
    h = D // 2
    E = router_w.shape[1]
    emb = hy_fw1.shape[1]
    bands = (emb - 1) // 2
    BL, BLc = B * L, B * Lc
    T = BL + BLc
    tm = BLc
    assert L % tm == 0 and L % FFT_N2 == 0 and Lc % SCAN_CHUNK == 0 and h % HEAD_DIM == 0 and B + 1 <= 8
    nlat = BL // tm
    ntile = T // tm
    tile_seg = jnp.concatenate([jnp.repeat(jnp.arange(B, dtype=i32), L // tm), jnp.full((1,), B, i32)])
    tmi = 2 * tm if L % (2 * tm) == 0 else tm
    T_pad = (T + tmi - 1) // tmi * tmi
    seg_in = jnp.minimum((jnp.arange(T_pad // tmi, dtype=i32) * tmi) // L, B)

    xs = jnp.concatenate([x.reshape(BL, D).astype(f32), ctx.reshape(BLc, D).astype(f32),
                          jnp.zeros((T_pad - T, D), f32)], axis=0)
    cond = jnp.concatenate([c.astype(f32), c_ctx.astype(f32)[None, :]], axis=0)
    mod = _modulation(cond, ada_w.astype(f32), ada_b.astype(f32))

    lb = jnp.cumsum(jax.nn.softmax(hg_lb_raw.astype(f32), axis=0), axis=0)
    lb = lb - lb[:1]

    dc = _dft_constants(L)
    zt_lat = _latent_ztab(L, bands, dc)
    zt_ctx = _ctx_ztab(Lc, bands)
    deltas = jnp.abs(jnp.linspace(HY_MIN_DECAY, HY_MAX_DECAY, 2 * h, dtype=f32)).reshape(1, 2 * h)

    w_in_b, p_hy_b, p_hg_b, w_out_b = (_to_bf16(w.astype(f32)) for w in (w_in, p_hy, p_hg, w_out))
    moe_w1_b, moe_w3_b, moe_w2_b = (_to_bf16(w.astype(f32)) for w in (moe_w1, moe_w3, moe_w2))

    rw_t = router_w.astype(f32).T
    rh = rw_t.astype(bf16)
    rl = (rw_t - rh.astype(f32)).astype(bf16)
    n_cls = N_GROUPS * len(_PAIRS)
    per = E // N_GROUPS
    cls_e = jnp.array([[per * g + a, per * g + b] for g in range(N_GROUPS) for (a, b) in _PAIRS], i32)

    for l in range(depth):
        last = l == depth - 1
        mod_l = mod[l]
        proj, lg = _inproj(xs, seg_in, mod_l, norm1_g[l].astype(f32).reshape(1, D), w_in_b, l, lb[l], tmi)
        p_hyena = hg = gates = proj
        o_f, o_b = _hgrn_scan(hg, lg, B, L, Lc)

        pad = EMB_PAD - emb
        wo = hy_fwout[l].astype(f32)
        wo_hi = wo.astype(bf16)
        wo_lo = (wo - wo_hi.astype(f32)).astype(bf16)
        fl = (jnp.pad(hy_fw1[l].astype(f32), ((0, pad), (0, 0))), hy_fb1[l].astype(f32).reshape(1, -1),
              hy_fw2[l].astype(f32), hy_fb2[l].astype(f32).reshape(1, -1),
              hy_fw3[l].astype(f32), hy_fb3[l].astype(f32).reshape(1, -1),
              jnp.concatenate([wo_hi, wo_hi, wo_lo], axis=0), hy_freq[l].astype(f32), deltas)
        u_lat, u_ctx = _shortconv(p_hyena, hy_conv_w[l].astype(f32), hy_conv_b[l].astype(f32), B, L, Lc)
        kf = _latent_filter(zt_lat, fl, dc, h)
        y_lat = _latent_hyena(u_lat, kf, hy_bias[l].astype(f32), dc, B, L)
        if last:
            y_ctx = jnp.zeros((BLc, h), f32)
            nt = nlat
        else:
            kk = _ctx_filter(zt_ctx, fl, Lc, h)
            y_ctx = _ctx_conv(u_ctx, kk, hy_bias[l].astype(f32), B, Lc)
            nt = ntile
        u = _merge(o_f, o_b, hg, hg_norm_g[l].astype(f32).reshape(1, HEAD_DIM), y_lat, y_ctx, gates,
                   p_hy_b, p_hg_b, l, tm, nt)
        g2 = norm2_g[l].astype(f32).reshape(1, D)
        x1, logits_t = _outproj(u, w_out_b, l, xs, tile_seg, mod_l, g2, rh, rl, tm, nt)

        Tm = nt * tm
        route = _route(logits_t, router_b.astype(f32), tm)
        cls = route[0].astype(i32)
        counts = jnp.sum((cls[:, None] == jnp.arange(n_cls, dtype=i32)[None, :]).astype(i32), axis=0)
        padded = (counts + MOE_ROWS - 1) // MOE_ROWS * MOE_ROWS
        pend = jnp.cumsum(padded)
        pstart = pend - padded
        sstart = jnp.cumsum(counts) - counts
        order = jnp.argsort(cls, stable=True).astype(i32)
        rank_of_tok = jnp.argsort(order).astype(i32)
        pos_of_tok = pstart[cls] + rank_of_tok - sstart[cls]
        P = (Tm + n_cls * (MOE_ROWS - 1) + MOE_ROWS - 1) // MOE_ROWS * MOE_ROWS
        nblk = P // MOE_ROWS
        blk_start = jnp.arange(nblk, dtype=i32) * MOE_ROWS
        blk_cls = jnp.minimum(jnp.sum((blk_start[:, None] >= pend[None, :]).astype(i32), axis=1), n_cls - 1)
        blk_e = cls_e[blk_cls].reshape(-1).astype(i32)
        nblk_used = (pend[-1] // MOE_ROWS).astype(i32).reshape(1)
        row_cls = jnp.repeat(blk_cls, MOE_ROWS)
        within = jnp.arange(P, dtype=i32) - pstart[row_cls]
        okf = (within < counts[row_cls]).astype(f32)
        src = order[jnp.clip(sstart[row_cls] + within, 0, Tm - 1)]
        seg_tok = jnp.repeat(tile_seg[:nt], tm).astype(f32)
        rw = jnp.stack([route[1][src] * okf, route[2][src] * okf, seg_tok[src]], axis=1)
        rw = jnp.pad(rw, ((0, 0), (0, 128 - 3)))
        xg = x1[src]
        yg = _moe(xg, rw, blk_e, nblk_used, mod_l, g2, moe_w1_b, moe_w3_b, moe_w2_b, l, B + 1)
        if last:
            xs = yg[pos_of_tok]
        else:
            xs = yg[jnp.concatenate([pos_of_tok, jnp.zeros((T_pad - T,), i32)])]

    out = _final_norm(xs[:BL], final_g.astype(f32).reshape(1, D), tm)
    return out.reshape(B, L, D).astype(x.dtype)
```
